```python
import jax, jax.numpy as jnp
from jax import lax
import numpy as np

D_MODEL = 1024
BATCH = 8
SEQ = 4096
DEPTH = 1

GRID_W = 64
ROPE_THETA = 10000.0
Q_BLOCK = 128
EPS = 1e-6

MLA_HEADS = 8
MLA_NOPE = 64
MLA_ROPE = 32
MLA_QK = MLA_NOPE + MLA_ROPE
MLA_V = 64
Q_LORA = 768
KV_LORA = 256

GQA_HEADS = 8
GQA_KV_HEADS = 2
GQA_HEAD_DIM = 64

MLA_WIDTH = MLA_HEADS * MLA_V
GQA_WIDTH = GQA_HEADS * GQA_HEAD_DIM

SPLIT_SIZES = (
    Q_LORA,
    KV_LORA,
    MLA_ROPE,
    GQA_HEADS * GQA_HEAD_DIM,
    GQA_KV_HEADS * GQA_HEAD_DIM,
    GQA_KV_HEADS * GQA_HEAD_DIM,
    D_MODEL,
    D_MODEL,
)
IN_WIDTH = sum(SPLIT_SIZES)
SPLIT_POINTS = tuple(int(v) for v in np.cumsum(SPLIT_SIZES)[:-1])

N_EXPERTS = 16
CAPACITY_FACTOR = 2
EXPERT_FF = 1024

kernel_name = "hybrid_mla_gqa_axial_ec_moe_encoder"


def rmsnorm(x, g):
    xf = x.astype(jnp.float32)
    y = xf * lax.rsqrt(jnp.mean(xf * xf, axis=-1, keepdims=True) + EPS)
    return (y * g.astype(jnp.float32)).astype(x.dtype)


def axial_angles(n, rot_dim):
    rows = n // GRID_W
    row = jnp.broadcast_to(jnp.arange(rows)[:, None], (rows, GRID_W)).reshape(n).astype(jnp.float32)
    col = jnp.broadcast_to(jnp.arange(GRID_W)[None, :], (rows, GRID_W)).reshape(n).astype(jnp.float32)
    nf = rot_dim // 4
    inv = ROPE_THETA ** (-jnp.arange(nf, dtype=jnp.float32) / nf)
    ang = jnp.concatenate([row[:, None] * inv, col[:, None] * inv], axis=-1)
    return jnp.cos(ang), jnp.sin(ang)


def apply_rope(x, cos, sin):
    r = x.shape[-1]
    xf = x.astype(jnp.float32).reshape(*x.shape[:-1], r // 2, 2)
    x1, x2 = xf[..., 0], xf[..., 1]
    c = cos[None, :, None, :]
    s = sin[None, :, None, :]
    out = jnp.stack([x1 * c - x2 * s, x1 * s + x2 * c], axis=-1)
    return out.reshape(x.shape).astype(x.dtype)


def blocked_attention(q, k, v):
    b, s, h, d = q.shape
    kvh = k.shape[2]
    g = h // kvh
    dv = v.shape[-1]
    nb = s // Q_BLOCK
    scale = d ** -0.5
    qb = q.reshape(b, nb, Q_BLOCK, kvh, g, d).transpose(1, 0, 2, 3, 4, 5)

    def one_block(qblk):
        sc = jnp.einsum('bqkgd,bskd->bkgqs', qblk, k, preferred_element_type=jnp.float32) * scale
        p = jax.nn.softmax(sc, axis=-1).astype(v.dtype)
        return jnp.einsum('bkgqs,bskd->bqkgd', p, v)

    out = lax.map(one_block, qb)
    return out.transpose(1, 0, 2, 3, 4, 5).reshape(b, s, h, dv)


def ec_moe_sequence(h, w_router, b_router, w_exp_gate, w_exp_up, w_exp_down):
    s, d = h.shape
    cap = CAPACITY_FACTOR * s // N_EXPERTS
    logits = jnp.einsum('sd,de->se', h, w_router, preferred_element_type=jnp.float32) + b_router.astype(jnp.float32)
    aff = jax.nn.softmax(logits, axis=-1)
    gate, idx = lax.top_k(aff.T, cap)
    xe = h[idx]
    a = jnp.einsum('ecd,edf->ecf', xe, w_exp_gate)
    u = jnp.einsum('ecd,edf->ecf', xe, w_exp_up)
    y = jnp.einsum('ecf,efd->ecd', jax.nn.silu(a) * u, w_exp_down)
    y = y * gate[..., None].astype(y.dtype)
    return jnp.zeros((s, d), h.dtype).at[idx.reshape(-1)].add(y.reshape(-1, d))


def setup_inputs(seed: int = 0) -> dict:
    key = jax.random.key(seed)
    ks = jax.random.split(key, 24)
    f32 = jnp.float32

    def w(k, shape, fan_in):
        return jax.random.normal(k, shape, f32) * (fan_in ** -0.5)

    def gain(k, n):
        return 1.0 + 0.05 * jax.random.normal(k, (n,), f32)

    return {
        "x": jax.random.normal(ks[0], (BATCH, SEQ, D_MODEL), f32),
        "g_attn_norm": gain(ks[1], D_MODEL),
        "w_in": w(ks[2], (D_MODEL, IN_WIDTH), D_MODEL),
        "b_gate": 0.02 * jax.random.normal(ks[3], (2 * D_MODEL,), f32),
        "g_q_lat": gain(ks[4], Q_LORA),
        "w_q_up": w(ks[5], (Q_LORA, MLA_HEADS * MLA_QK), Q_LORA),
        "g_kv_lat": gain(ks[6], KV_LORA),
        "w_kv_up": w(ks[7], (KV_LORA, MLA_HEADS * (MLA_NOPE + MLA_V)), KV_LORA),
        "g_mla_qnorm": gain(ks[8], MLA_QK),
        "g_mla_knorm": gain(ks[9], MLA_QK),
        "g_gqa_qnorm": gain(ks[10], GQA_HEAD_DIM),
        "g_gqa_knorm": gain(ks[11], GQA_HEAD_DIM),
        "w_mla_branch": w(ks[12], (MLA_WIDTH, D_MODEL), MLA_WIDTH),
        "w_gqa_branch": w(ks[13], (GQA_WIDTH, D_MODEL), GQA_WIDTH),
        "w_out": w(ks[14], (D_MODEL, D_MODEL), D_MODEL),
        "g_ffn_norm": gain(ks[15], D_MODEL),
        "w_router": w(ks[16], (D_MODEL, N_EXPERTS), D_MODEL),
        "b_router": 0.01 * jax.random.normal(ks[17], (N_EXPERTS,), f32),
        "w_exp_gate": w(ks[18], (N_EXPERTS, D_MODEL, EXPERT_FF), D_MODEL),
        "w_exp_up": w(ks[19], (N_EXPERTS, D_MODEL, EXPERT_FF), D_MODEL),
        "w_exp_down": w(ks[20], (N_EXPERTS, EXPERT_FF, D_MODEL), EXPERT_FF),
    }


def reference(x, g_attn_norm, w_in, b_gate, g_q_lat, w_q_up, g_kv_lat, w_kv_up,
              g_mla_qnorm, g_mla_knorm, g_gqa_qnorm, g_gqa_knorm,
              w_mla_branch, w_gqa_branch, w_out, g_ffn_norm,
              w_router, b_router, w_exp_gate, w_exp_up, w_exp_down):
    b, s, _ = x.shape
    cos_mla, sin_mla = axial_angles(s, MLA_ROPE)
    cos_gqa, sin_gqa = axial_angles(s, GQA_HEAD_DIM)

    for _ in range(DEPTH):
        h = rmsnorm(x, g_attn_norm)
        proj = jnp.einsum('bsd,dn->bsn', h, w_in)
        q_lat, kv_lat, k_rope, q_g, k_g, v_g, gate_a, gate_b = jnp.split(proj, SPLIT_POINTS, axis=-1)

        c_q = rmsnorm(q_lat, g_q_lat)
        q_a = jnp.einsum('bsl,ln->bsn', c_q, w_q_up).reshape(b, s, MLA_HEADS, MLA_QK)
        c_kv = rmsnorm(kv_lat, g_kv_lat)
        kv_a = jnp.einsum('bsl,ln->bsn', c_kv, w_kv_up).reshape(b, s, MLA_HEADS, MLA_NOPE + MLA_V)
        k_nope, v_a = kv_a[..., :MLA_NOPE], kv_a[..., MLA_NOPE:]
        k_r = jnp.broadcast_to(k_rope[:, :, None, :], (b, s, MLA_HEADS, MLA_ROPE))
        k_a = jnp.concatenate([k_nope, k_r], axis=-1)
        q_a = rmsnorm(q_a, g_mla_qnorm)
        k_a = rmsnorm(k_a, g_mla_knorm)
        q_a = jnp.concatenate([q_a[..., :MLA_NOPE], apply_rope(q_a[..., MLA_NOPE:], cos_mla, sin_mla)], axis=-1)
        k_a = jnp.concatenate([k_a[..., :MLA_NOPE], apply_rope(k_a[..., MLA_NOPE:], cos_mla, sin_mla)], axis=-1)
        y_a = blocked_attention(q_a, k_a, v_a).reshape(b, s, MLA_WIDTH)

        q_b = rmsnorm(q_g.reshape(b, s, GQA_HEADS, GQA_HEAD_DIM), g_gqa_qnorm)
        k_b = rmsnorm(k_g.reshape(b, s, GQA_KV_HEADS, GQA_HEAD_DIM), g_gqa_knorm)
        v_b = v_g.reshape(b, s, GQA_KV_HEADS, GQA_HEAD_DIM)
        q_b = apply_rope(q_b, cos_gqa, sin_gqa)
        k_b = apply_rope(k_b, cos_gqa, sin_gqa)
        y_b = blocked_attention(q_b, k_b, v_b).reshape(b, s, GQA_WIDTH)

        ga = jax.nn.sigmoid(gate_a + b_gate[:D_MODEL])
        gb = jax.nn.sigmoid(gate_b + b_gate[D_MODEL:])
        merged = (ga * jnp.einsum('bsm,md->bsd', y_a, w_mla_branch)
                  + gb * jnp.einsum('bsm,md->bsd', y_b, w_gqa_branch))
        x = x + jnp.einsum('bsd,de->bse', merged, w_out)

        h2 = rmsnorm(x, g_ffn_norm)
        moe = jax.vmap(ec_moe_sequence, in_axes=(0, None, None, None, None, None))(
            h2, w_router, b_router, w_exp_gate, w_exp_up, w_exp_down)
        x = x + moe
    return x
```

```python
import functools

import jax
import jax.numpy as jnp
import numpy as np
from jax import lax
from jax.experimental import pallas as pl
from jax.experimental.pallas import tpu as pltpu

F32 = jnp.float32
BF16 = jnp.bfloat16
I32 = jnp.int32

D_MODEL = 1024
GRID_W = 64
ROPE_THETA = 10000.0
EPS = 1e-6

MLA_HEADS = 8
MLA_NOPE = 64
MLA_ROPE = 32
MLA_QK = MLA_NOPE + MLA_ROPE
MLA_V = 64
Q_LORA = 768
KV_LORA = 256

GQA_HEADS = 8
GQA_KV_HEADS = 2
GQA_HEAD_DIM = 64
GQA_GROUP = GQA_HEADS // GQA_KV_HEADS

N_EXPERTS = 16
CAPACITY_FACTOR = 2
EXPERT_FF = 1024

MIN_NORMAL_BITS = 0x00800000
LANES = 128
HEADS_WIDTH = MLA_HEADS * LANES

_C_QLAT = 0
_C_KVLAT = _C_QLAT + Q_LORA
_C_KROPE = _C_KVLAT + KV_LORA
_C_QG = _C_KROPE + LANES
_C_KG = _C_QG + HEADS_WIDTH
_C_VG = _C_KG + GQA_KV_HEADS * LANES
_C_GA = _C_VG + GQA_KV_HEADS * LANES
_C_GB = _C_GA + D_MODEL
_C_END = _C_GB + D_MODEL

MLA_ONES_LANE = 0
GQA_ONES_LANE = GQA_HEAD_DIM
LOG2_E = 1.4426950408889634

PRE_TM = 256
ATT_TQ = 256
ATT_UNITS = 8
ATT_TK = 512
ATT_TK2 = 2048
ROUTE_BLK = 512
COMB_TC = 256
VMEM_LIMIT = 56 * 1024 * 1024

_NT = (((1,), (1,)), ((), ()))


def _dot(a, b):
    return jnp.dot(a, b, preferred_element_type=F32)


def _rms(x, width):
    return x * lax.rsqrt(jnp.sum(x * x, axis=-1, keepdims=True) * (1.0 / width) + EPS)


def _rope(x, c, sa, sb):
    return x * c + pltpu.roll(x, LANES - 1, 1) * sa + pltpu.roll(x, 1, 1) * sb


def _pre_kernel(x_ref, gattn_ref, win_ref, gql_ref, wqup_ref, gkvl_ref, wkvup_ref,
                gq_ref, gk_ref, ggq_ref, ggk_ref, bgate_ref,
                cm_ref, sam_ref, sbm_ref, cg_ref, sag_ref, sbg_ref,
                qa_ref, ka_ref, va_ref, qb_ref, kb_ref, vb_ref, ga_ref, gb_ref):
    x = x_ref[...]
    h = (_rms(x, D_MODEL) * gattn_ref[...]).astype(BF16)

    def proj(lo, hi):
        return _dot(h, win_ref[:, lo:hi])

    lane = lax.broadcasted_iota(I32, (x.shape[0], LANES), 1)
    low_half = lane < GQA_HEAD_DIM
    cm, sam, sbm = cm_ref[...], sam_ref[...], sbm_ref[...]
    cg, sag, sbg = cg_ref[...], sag_ref[...], sbg_ref[...]

    c_q = (_rms(proj(_C_QLAT, _C_KVLAT), Q_LORA) * gql_ref[...]).astype(BF16)
    gq = gq_ref[...]
    q_scale = MLA_QK ** -0.5 * LOG2_E
    for hd in range(MLA_HEADS):
        seg = _dot(c_q, wqup_ref[:, hd * LANES:(hd + 1) * LANES])
        seg = _rms(seg, MLA_QK) * gq
        seg = _rope(seg, cm, sam, sbm) * q_scale
        qa_ref[:, hd * LANES:(hd + 1) * LANES] = seg.astype(BF16)

    c_kv = (_rms(proj(_C_KVLAT, _C_KROPE), KV_LORA) * gkvl_ref[...]).astype(BF16)
    k_rope = proj(_C_KROPE, _C_QG)
    gk = gk_ref[...]
    for hd in range(MLA_HEADS):
        kv = _dot(c_kv, wkvup_ref[:, hd * LANES:(hd + 1) * LANES])
        kseg = jnp.where(low_half, kv, 0.0) + k_rope
        kseg = _rms(kseg, MLA_QK) * gk
        ka_ref[:, hd * LANES:(hd + 1) * LANES] = _rope(kseg, cm, sam, sbm).astype(BF16)
        vseg = jnp.where(lane == MLA_ONES_LANE, 1.0, jnp.where(low_half, 0.0, kv))
        va_ref[:, hd * LANES:(hd + 1) * LANES] = vseg.astype(BF16)

    ggq = ggq_ref[...]
    qg_scale = GQA_HEAD_DIM ** -0.5 * LOG2_E
    for hd in range(GQA_HEADS):
        seg = proj(_C_QG + hd * LANES, _C_QG + (hd + 1) * LANES)
        seg = _rms(seg, GQA_HEAD_DIM) * ggq
        seg = _rope(seg, cg, sag, sbg) * qg_scale
        qb_ref[:, hd * LANES:(hd + 1) * LANES] = seg.astype(BF16)
    ggk = ggk_ref[...]
    for hd in range(GQA_KV_HEADS):
        seg = proj(_C_KG + hd * LANES, _C_KG + (hd + 1) * LANES)
        seg = _rms(seg, GQA_HEAD_DIM) * ggk
        kb_ref[:, hd * LANES:(hd + 1) * LANES] = _rope(seg, cg, sag, sbg).astype(BF16)
        vseg = proj(_C_VG + hd * LANES, _C_VG + (hd + 1) * LANES)
        vseg = jnp.where(lane == GQA_ONES_LANE, 1.0, vseg)
        vb_ref[:, hd * LANES:(hd + 1) * LANES] = vseg.astype(BF16)

    bg = bgate_ref[...]
    ga_ref[...] = jax.nn.sigmoid(proj(_C_GA, _C_GB) + bg[:, :D_MODEL]).astype(BF16)
    gb_ref[...] = jax.nn.sigmoid(proj(_C_GB, _C_END) + bg[:, D_MODEL:]).astype(BF16)


def _pre_call(x2d, seq, gattn, win, gql, wqup, gkvl, wkvup, gq, gk, ggq, ggk, bgate, tabs):
    t = x2d.shape[0]
    tm = PRE_TM
    n_seq_tiles = seq // tm

    def full(a):
        return pl.BlockSpec(a.shape, lambda i: (0,) * a.ndim)

    def tok(width):
        return pl.BlockSpec((tm, width), lambda i: (i, 0))

    tab_spec = pl.BlockSpec((tm, LANES), lambda i: (i % n_seq_tiles, 0))
    params = (gattn, win, gql, wqup, gkvl, wkvup, gq, gk, ggq, ggk, bgate)
    kv_w = GQA_KV_HEADS * LANES
    out_widths = (HEADS_WIDTH, HEADS_WIDTH, HEADS_WIDTH, HEADS_WIDTH, kv_w, kv_w, D_MODEL, D_MODEL)
    return pl.pallas_call(
        _pre_kernel,
        grid=(t // tm,),
        in_specs=[tok(D_MODEL)] + [full(a) for a in params] + [tab_spec] * 6,
        out_specs=[tok(w) for w in out_widths],
        out_shape=[jax.ShapeDtypeStruct((t, w), BF16) for w in out_widths],
        compiler_params=pltpu.CompilerParams(
            dimension_semantics=("arbitrary",), vmem_limit_bytes=VMEM_LIMIT),
        name="pre_attention",
    )(x2d, *params, *tabs)


def _flash_kernel(q_ref, k_ref, v_ref, o_ref, s0_ref, s1_ref, *, units, ones_lane):
    tq = s0_ref.shape[0]
    s_len = k_ref.shape[1]
    s_bufs = (s0_ref, s1_ref)

    def q_of(u):
        r, g = units[u]
        return q_ref[0, r * tq:(r + 1) * tq, g * LANES:(g + 1) * LANES]

    def score_pass(u, j, q, m128):
        cols = slice(j * ATT_TK, (j + 1) * ATT_TK)
        s = lax.dot_general(q, k_ref[0, cols, :], _NT, preferred_element_type=F32)
        s_bufs[u % 2][:, cols] = s
        for c in range(ATT_TK // LANES):
            m128 = jnp.maximum(m128, s[:, c * LANES:(c + 1) * LANES])
        return m128

    def value_pass(u, j, m, acc):
        cols = slice(j * ATT_TK2, (j + 1) * ATT_TK2)
        p = jnp.exp2(s_bufs[u % 2][:, cols] - m)
        return acc + _dot(p.astype(BF16), v_ref[0, cols, :])

    neg = jnp.full((tq, LANES), -1e30, F32)
    m128 = neg
    for j in range(s_len // ATT_TK):
        m128 = score_pass(0, j, q_of(0), m128)
    ratio = ATT_TK2 // ATT_TK
    for u in range(len(units)):
        m = jnp.max(m128, axis=-1, keepdims=True)
        acc = jnp.zeros((tq, LANES), F32)
        has_next = u + 1 < len(units)
        if has_next:
            q_next = q_of(u + 1)
            m128 = neg
        for j2 in range(s_len // ATT_TK2):
            acc = value_pass(u, j2, m, acc)
            if has_next:
                for j in range(j2 * ratio, (j2 + 1) * ratio):
                    m128 = score_pass(u + 1, j, q_next, m128)
        row_sum = acc[:, ones_lane:ones_lane + 1]
        r, g = units[u]
        o_ref[0, r * tq:(r + 1) * tq, g * LANES:(g + 1) * LANES] = (acc / row_sum).astype(BF16)


def _flash_call(q, k, v, group, ones_lane, name):
    b, s, qw = q.shape
    n_kv = k.shape[2] // LANES
    assert qw == n_kv * group * LANES and ATT_UNITS % group == 0
    assert s % ATT_TK2 == 0 and ATT_TK2 % ATT_TK == 0
    row_tiles = ATT_UNITS // group
    rows = ATT_TQ * row_tiles
    assert s % rows == 0
    units = tuple((r, g) for r in range(row_tiles) for g in range(group))
    q_spec = pl.BlockSpec((1, rows, group * LANES), lambda bi, kh, qi: (bi, qi, kh))
    kv_spec = pl.BlockSpec((1, s, LANES), lambda bi, kh, qi: (bi, 0, kh))
    return pl.pallas_call(
        functools.partial(_flash_kernel, units=units, ones_lane=ones_lane),
        grid=(b, n_kv, s // rows),
        in_specs=[q_spec, kv_spec, kv_spec],
        out_specs=q_spec,
        out_shape=jax.ShapeDtypeStruct(q.shape, BF16),
        scratch_shapes=[pltpu.VMEM((ATT_TQ, s), F32), pltpu.VMEM((ATT_TQ, s), F32)],
        compiler_params=pltpu.CompilerParams(
            dimension_semantics=("arbitrary", "arbitrary", "arbitrary"),
            vmem_limit_bytes=VMEM_LIMIT),
        name=name,
    )(q, k, v)


def _post_kernel(x_ref, ya_ref, yb_ref, ga_ref, gb_ref, wa_ref, wb_ref, wout_ref,
                 gffn_ref, wrh_ref, wrl_ref, br_ref, x1_ref, h2_ref, aff_ref):
    a = _dot(ya_ref[...], wa_ref[...])
    b = _dot(yb_ref[...], wb_ref[...])
    merged = ga_ref[...].astype(F32) * a + gb_ref[...].astype(F32) * b
    x1 = x_ref[...] + _dot(merged.astype(BF16), wout_ref[...])
    x1_ref[...] = x1
    h2 = _rms(x1, D_MODEL) * gffn_ref[...]
    h2_hi = h2.astype(BF16)
    h2_ref[...] = h2_hi
    h2_lo = (h2 - h2_hi.astype(F32)).astype(BF16)
    wrh = wrh_ref[...]
    logits = _dot(h2_hi, wrh) + _dot(h2_lo, wrh) + _dot(h2_hi, wrl_ref[...]) + br_ref[...]
    e = jnp.exp(logits - jnp.max(logits, axis=-1, keepdims=True))
    aff_ref[...] = e / jnp.sum(e, axis=-1, keepdims=True)


def _post_call(x2d, ya, yb, ga, gb, wa, wb, wout, gffn, wrh, wrl, br):
    t = x2d.shape[0]
    tm = PRE_TM

    def full(a):
        return pl.BlockSpec(a.shape, lambda i: (0,) * a.ndim)

    def tok(width):
        return pl.BlockSpec((tm, width), lambda i: (i, 0))

    params = (wa, wb, wout, gffn, wrh, wrl, br)
    return pl.pallas_call(
        _post_kernel,
        grid=(t // tm,),
        in_specs=[tok(D_MODEL)] * 5 + [full(a) for a in params],
        out_specs=[tok(D_MODEL), tok(D_MODEL), tok(LANES)],
        out_shape=[jax.ShapeDtypeStruct((t, D_MODEL), F32),
                   jax.ShapeDtypeStruct((t, D_MODEL), BF16),
                   jax.ShapeDtypeStruct((t, LANES), F32)],
        compiler_params=pltpu.CompilerParams(
            dimension_semantics=("arbitrary",), vmem_limit_bytes=VMEM_LIMIT),
        name="post_attention",
    )(x2d, ya, yb, ga, gb, *params)


def _route_kernel(aff_ref, slot_t_ref, gate_t_ref, slot_r_ref, *, cap):
    s = aff_ref.shape[1]

    def count(mask):
        return jnp.sum(jnp.where(mask, 1.0, 0.0), axis=0, keepdims=True)

    def as_float(bits):
        return lax.bitcast_convert_type(bits, F32)

    def search(i, bits):
        cand = bits | jnp.left_shift(jnp.int32(1), 29 - i)
        return jnp.where(count(aff_ref[0] >= as_float(cand)) >= cap, cand, bits)

    bits = lax.fori_loop(0, 30, search, jnp.zeros((1, LANES), I32))
    is_normal = bits >= MIN_NORMAL_BITS
    thr = jnp.where(is_normal, as_float(bits), 0.0)
    thr_next = as_float(jnp.where(is_normal, bits + 1, MIN_NORMAL_BITS))
    need = cap - count(aff_ref[0] >= thr_next)

    blk = ROUTE_BLK
    tri = (lax.broadcasted_iota(I32, (blk, blk), 0) >= lax.broadcasted_iota(I32, (blk, blk), 1)).astype(BF16)
    eye = (lax.broadcasted_iota(I32, (LANES, LANES), 0) == lax.broadcasted_iota(I32, (LANES, LANES), 1)).astype(BF16)
    carry_eq = jnp.zeros((1, LANES), F32)
    carry_sel = jnp.zeros((1, LANES), F32)
    for c in range(s // blk):
        rows = slice(c * blk, (c + 1) * blk)
        ab = aff_ref[0, rows, :]
        gt = ab >= thr_next
        eq = jnp.where((ab >= thr) & jnp.logical_not(gt), 1.0, 0.0)
        eq_incl = _dot(tri, eq.astype(BF16)) + carry_eq
        carry_eq = eq_incl[blk - 1:blk, :]
        sel = jnp.where(gt | ((eq > 0.0) & (eq_incl - eq < need)), 1.0, 0.0)
        sel_incl = _dot(tri, sel.astype(BF16)) + carry_sel
        carry_sel = sel_incl[blk - 1:blk, :]
        slot1 = sel * sel_incl
        slot_t_ref[0, rows, :] = slot1.astype(I32) - 1
        gate_t_ref[0, rows, :] = sel * aff_ref[0, rows, :]
        hi = jnp.floor(slot1 * (1.0 / 32.0))
        lo = slot1 - 32.0 * hi
        rows_hi = lax.dot_general(eye, hi.astype(BF16), _NT, preferred_element_type=F32)
        rows_lo = lax.dot_general(eye, lo.astype(BF16), _NT, preferred_element_type=F32)
        slot_rows = (32.0 * rows_hi + rows_lo).astype(I32) - 1
        slot_r_ref[0, :, rows] = slot_rows[:N_EXPERTS]


def _route_call(aff, cap):
    b, s, _ = aff.shape
    tok_spec = pl.BlockSpec((1, s, LANES), lambda bi: (bi, 0, 0))
    return pl.pallas_call(
        functools.partial(_route_kernel, cap=cap),
        grid=(b,),
        in_specs=[tok_spec],
        out_specs=[tok_spec, tok_spec, pl.BlockSpec((1, N_EXPERTS, s), lambda bi: (bi, 0, 0))],
        out_shape=[jax.ShapeDtypeStruct((b, s, LANES), I32),
                   jax.ShapeDtypeStruct((b, s, LANES), F32),
                   jax.ShapeDtypeStruct((b, N_EXPERTS, s), I32)],
        compiler_params=pltpu.CompilerParams(
            dimension_semantics=("arbitrary",), vmem_limit_bytes=VMEM_LIMIT),
        name="route",
    )(aff)


def _expert_kernel(slot_r_ref, h2_ref, wg_ref, wu_ref, wd_ref, y_ref, *, cap):
    e = pl.program_id(1)
    s = h2_ref.shape[1]
    blk = ROUTE_BLK
    slot_row = slot_r_ref[0, pl.ds(e, 1), :]
    slot_ids = lax.broadcasted_iota(I32, (cap, blk), 0)
    xe = jnp.zeros((cap, D_MODEL), F32)
    for c in range(s // blk):
        pick = jnp.where(slot_ids == slot_row[:, c * blk:(c + 1) * blk], 1.0, 0.0).astype(BF16)
        xe = xe + _dot(pick, h2_ref[0, c * blk:(c + 1) * blk, :])
    xe = xe.astype(BF16)
    a = _dot(xe, wg_ref[0])
    u = _dot(xe, wu_ref[0])
    act = (a * jax.nn.sigmoid(a) * u).astype(BF16)
    y_ref[0] = _dot(act, wd_ref[0]).astype(BF16)


def _expert_call(slot_r, h2, wg, wu, wd, cap):
    b, s, d = h2.shape
    w_spec = pl.BlockSpec((1, d, EXPERT_FF), lambda bi, e: (e, 0, 0))
    return pl.pallas_call(
        functools.partial(_expert_kernel, cap=cap),
        grid=(b, N_EXPERTS),
        in_specs=[pl.BlockSpec((1, N_EXPERTS, s), lambda bi, e: (bi, 0, 0)),
                  pl.BlockSpec((1, s, d), lambda bi, e: (bi, 0, 0)),
                  w_spec, w_spec,
                  pl.BlockSpec((1, EXPERT_FF, d), lambda bi, e: (e, 0, 0))],
        out_specs=pl.BlockSpec((1, cap, d), lambda bi, e: (bi, e, 0)),
        out_shape=jax.ShapeDtypeStruct((b, N_EXPERTS * cap, d), BF16),
        compiler_params=pltpu.CompilerParams(
            dimension_semantics=("arbitrary", "arbitrary"), vmem_limit_bytes=VMEM_LIMIT),
        name="experts",
    )(slot_r, h2, wg, wu, wd)


def _combine_kernel(x1_ref, slot_t_ref, gate_t_ref, y_ref, o_ref, *, cap):
    acc = x1_ref[0]
    tc = acc.shape[0]
    slot_t = slot_t_ref[0]
    gate_t = gate_t_ref[0]
    slot_ids = lax.broadcasted_iota(I32, (tc, cap), 1)
    for e in range(N_EXPERTS):
        place = jnp.where(slot_ids == slot_t[:, e:e + 1], 1.0, 0.0).astype(BF16)
        acc = acc + gate_t[:, e:e + 1] * _dot(place, y_ref[0, e * cap:(e + 1) * cap, :])
    o_ref[0] = acc


def _combine_call(x1, slot_t, gate_t, y, cap):
    b, s, d = x1.shape
    tc = COMB_TC
    return pl.pallas_call(
        functools.partial(_combine_kernel, cap=cap),
        grid=(b, s // tc),
        in_specs=[pl.BlockSpec((1, tc, d), lambda bi, ci: (bi, ci, 0)),
                  pl.BlockSpec((1, tc, LANES), lambda bi, ci: (bi, ci, 0)),
                  pl.BlockSpec((1, tc, LANES), lambda bi, ci: (bi, ci, 0)),
                  pl.BlockSpec((1, N_EXPERTS * cap, d), lambda bi, ci: (bi, 0, 0))],
        out_specs=pl.BlockSpec((1, tc, d), lambda bi, ci: (bi, ci, 0)),
        out_shape=jax.ShapeDtypeStruct((b, s, d), F32),
        compiler_params=pltpu.CompilerParams(
            dimension_semantics=("arbitrary", "arbitrary"), vmem_limit_bytes=VMEM_LIMIT),
        name="combine",
    )(x1, slot_t, gate_t, y)


def _rope_tables(n, rot_dim, lane_lo, lane_hi, period):
    rows = n // GRID_W
    row = jnp.broadcast_to(jnp.arange(rows)[:, None], (rows, GRID_W)).reshape(n).astype(F32)
    col = jnp.broadcast_to(jnp.arange(GRID_W)[None, :], (rows, GRID_W)).reshape(n).astype(F32)
    nf = rot_dim // 4
    inv = ROPE_THETA ** (-jnp.arange(nf, dtype=F32) / nf)
    ang = jnp.concatenate([row[:, None] * inv, col[:, None] * inv], axis=-1)
    cos, sin = jnp.cos(ang), jnp.sin(ang)
    lane = np.arange(LANES)
    active = (lane >= lane_lo) & (lane < lane_hi)
    pair = np.where(active, ((lane - lane_lo) % period) // 2, 0)
    even = (lane % 2) == 0
    c = jnp.where(active[None, :], cos[:, pair], 1.0)
    sa = jnp.where((active & even)[None, :], -sin[:, pair], 0.0)
    sb = jnp.where((active & ~even)[None, :], sin[:, pair], 0.0)
    return c, sa, sb


def _pad_cols(w, heads, width):
    k = w.shape[0]
    w = w.reshape(k, heads, width)
    return jnp.pad(w, ((0, 0), (0, 0), (0, LANES - width))).reshape(k, heads * LANES)


def kernel(x, g_attn_norm, w_in, b_gate, g_q_lat, w_q_up, g_kv_lat, w_kv_up, g_mla_qnorm, g_mla_knorm, g_gqa_qnorm, g_gqa_knorm, w_mla_branch, w_gqa_branch, w_out, g_ffn_norm, w_router, b_router, w_exp_gate, w_exp_up, w_exp_down):
    b, s, d = x.shape
    assert d == D_MODEL and s % max(PRE_TM, ATT_TQ, ATT_TK, ROUTE_BLK, COMB_TC) == 0
    cap = CAPACITY_FACTOR * s // N_EXPERTS
    t = b * s

    splits = np.cumsum([Q_LORA, KV_LORA, MLA_ROPE, GQA_HEADS * GQA_HEAD_DIM,
                        GQA_KV_HEADS * GQA_HEAD_DIM, GQA_KV_HEADS * GQA_HEAD_DIM, D_MODEL])
    w_ql, w_kvl, w_kr, w_qg, w_kg, w_vg, w_ga, w_gb = jnp.split(w_in, splits, axis=1)
    w_kr = jnp.pad(w_kr, ((0, 0), (MLA_NOPE, LANES - MLA_QK)))
    w_qg = _pad_cols(w_qg, GQA_HEADS, GQA_HEAD_DIM)
    w_kg = _pad_cols(w_kg, GQA_KV_HEADS, GQA_HEAD_DIM)
    w_vg = _pad_cols(w_vg, GQA_KV_HEADS, GQA_HEAD_DIM)
    win = jnp.concatenate([w_ql, w_kvl, w_kr, w_qg, w_kg, w_vg, w_ga, w_gb], axis=1).astype(BF16)
    wqup = _pad_cols(w_q_up, MLA_HEADS, MLA_QK).astype(BF16)
    wkvup = w_kv_up.astype(BF16)

    def row(v):
        return v.reshape(1, -1).astype(F32)

    gq = row(jnp.pad(g_mla_qnorm, (0, LANES - MLA_QK)))
    gk = row(jnp.pad(g_mla_knorm, (0, LANES - MLA_QK)))
    ggq = row(jnp.pad(g_gqa_qnorm, (0, LANES - GQA_HEAD_DIM)))
    ggk = row(jnp.pad(g_gqa_knorm, (0, LANES - GQA_HEAD_DIM)))

    wa = jnp.pad(w_mla_branch.reshape(MLA_HEADS, MLA_V, d), ((0, 0), (LANES - MLA_V, 0), (0, 0)))
    wa = wa.reshape(HEADS_WIDTH, d).astype(BF16)
    wb = jnp.pad(w_gqa_branch.reshape(GQA_HEADS, GQA_HEAD_DIM, d), ((0, 0), (0, LANES - GQA_HEAD_DIM), (0, 0)))
    wb = wb.reshape(HEADS_WIDTH, d).astype(BF16)

    wr = jnp.pad(w_router, ((0, 0), (0, LANES - N_EXPERTS)))
    wr_hi = wr.astype(BF16)
    wr_lo = (wr - wr_hi.astype(F32)).astype(BF16)
    br = jnp.concatenate([b_router.astype(F32), jnp.full((LANES - N_EXPERTS,), -1e30, F32)]).reshape(1, LANES)

    tabs = (_rope_tables(s, MLA_ROPE, MLA_NOPE, MLA_QK, MLA_ROPE)
            + _rope_tables(s, GQA_HEAD_DIM, 0, GQA_HEAD_DIM, GQA_HEAD_DIM))

    x2d = x.reshape(t, d)
    qa, ka, va, qb, kb, vb, ga, gb = _pre_call(
        x2d, s, row(g_attn_norm), win, row(g_q_lat), wqup, row(g_kv_lat), wkvup,
        gq, gk, ggq, ggk, row(b_gate), tabs)

    def seq(a):
        return a.reshape(b, s, a.shape[-1])

    ya = _flash_call(seq(qa), seq(ka), seq(va), 1, MLA_ONES_LANE, "attention_mla")
    yb = _flash_call(seq(qb), seq(kb), seq(vb), GQA_GROUP, GQA_ONES_LANE, "attention_gqa")

    x1, h2, aff = _post_call(x2d, ya.reshape(t, HEADS_WIDTH), yb.reshape(t, HEADS_WIDTH), ga, gb,
                             wa, wb, w_out.astype(BF16), row(g_ffn_norm), wr_hi, wr_lo, br)

    slot_t, gate_t, slot_r = _route_call(seq(aff), cap)
    y = _expert_call(slot_r, seq(h2), w_exp_gate.astype(BF16), w_exp_up.astype(BF16),
                     w_exp_down.astype(BF16), cap)
    return _combine_call(seq(x1), slot_t, gate_t, y, cap)
```

```python
import functools

import jax
import jax.numpy as jnp
import numpy as np
from jax import lax
from jax.experimental import pallas as pl
from jax.experimental.pallas import tpu as pltpu

F32 = jnp.float32
BF16 = jnp.bfloat16
I32 = jnp.int32

D_MODEL = 1024
GRID_W = 64
ROPE_THETA = 10000.0
EPS = 1e-6

MLA_HEADS = 8
MLA_NOPE = 64
MLA_ROPE = 32
MLA_QK = MLA_NOPE + MLA_ROPE
MLA_V = 64
Q_LORA = 768
KV_LORA = 256

GQA_HEADS = 8
GQA_KV_HEADS = 2
GQA_HEAD_DIM = 64
GQA_GROUP = GQA_HEADS // GQA_KV_HEADS

N_EXPERTS = 16
CAPACITY_FACTOR = 2
EXPERT_FF = 1024

MIN_NORMAL_BITS = 0x00800000
LANES = 128
HEADS_WIDTH = MLA_HEADS * LANES

_C_QLAT = 0
_C_KVLAT = _C_QLAT + Q_LORA
_C_KROPE = _C_KVLAT + KV_LORA
_C_QG = _C_KROPE + LANES
_C_KG = _C_QG + HEADS_WIDTH
_C_VG = _C_KG + GQA_KV_HEADS * LANES
_C_GA = _C_VG + GQA_KV_HEADS * LANES
_C_GB = _C_GA + D_MODEL
_C_END = _C_GB + D_MODEL

MLA_ONES_LANE = 0
GQA_ONES_LANE = GQA_HEAD_DIM
LOG2_E = 1.4426950408889634

PRE_TM = 256
ATT_TQ = 256
ATT_UNITS = 8
ATT_TK = 512
ATT_TK2 = 2048
ROUTE_BLK = 512
MOE_TC = 256
MOE_WIN = 128
SLOT_ALIGN = 16
META_FLAG_LANE = N_EXPERTS
VMEM_LIMIT = 56 * 1024 * 1024

_NT = (((1,), (1,)), ((), ()))


def _dot(a, b):
    return jnp.dot(a, b, preferred_element_type=F32)


def _rms(x, width):
    return x * lax.rsqrt(jnp.sum(x * x, axis=-1, keepdims=True) * (1.0 / width) + EPS)


def _rope(x, c, sa, sb):
    return x * c + pltpu.roll(x, LANES - 1, 1) * sa + pltpu.roll(x, 1, 1) * sb


def _pre_kernel(x_ref, gattn_ref, win_ref, gql_ref, wqup_ref, gkvl_ref, wkvup_ref,
                gq_ref, gk_ref, ggq_ref, ggk_ref, bgate_ref,
                cm_ref, sam_ref, sbm_ref, cg_ref, sag_ref, sbg_ref,
                qa_ref, ka_ref, va_ref, qb_ref, kb_ref, vb_ref, ga_ref, gb_ref):
    x = x_ref[...]
    h = (_rms(x, D_MODEL) * gattn_ref[...]).astype(BF16)

    def proj(lo, hi):
        return _dot(h, win_ref[:, lo:hi])

    lane = lax.broadcasted_iota(I32, (x.shape[0], LANES), 1)
    low_half = lane < GQA_HEAD_DIM
    cm, sam, sbm = cm_ref[...], sam_ref[...], sbm_ref[...]
    cg, sag, sbg = cg_ref[...], sag_ref[...], sbg_ref[...]

    latents = proj(_C_QLAT, _C_QG)
    c_q = (_rms(latents[:, _C_QLAT:_C_KVLAT], Q_LORA) * gql_ref[...]).astype(BF16)
    q_up = _dot(c_q, wqup_ref[...])
    gq = gq_ref[...]
    q_scale = MLA_QK ** -0.5 * LOG2_E
    for hd in range(MLA_HEADS):
        seg = _rms(q_up[:, hd * LANES:(hd + 1) * LANES], MLA_QK) * gq
        seg = _rope(seg, cm, sam, sbm) * q_scale
        qa_ref[:, hd * LANES:(hd + 1) * LANES] = seg.astype(BF16)

    c_kv = (_rms(latents[:, _C_KVLAT:_C_KROPE], KV_LORA) * gkvl_ref[...]).astype(BF16)
    k_rope = latents[:, _C_KROPE:_C_QG]
    kv_up = _dot(c_kv, wkvup_ref[...])
    gk = gk_ref[...]
    for hd in range(MLA_HEADS):
        kv = kv_up[:, hd * LANES:(hd + 1) * LANES]
        kseg = jnp.where(low_half, kv, 0.0) + k_rope
        kseg = _rms(kseg, MLA_QK) * gk
        ka_ref[:, hd * LANES:(hd + 1) * LANES] = _rope(kseg, cm, sam, sbm).astype(BF16)
        vseg = jnp.where(lane == MLA_ONES_LANE, 1.0, jnp.where(low_half, 0.0, kv))
        va_ref[:, hd * LANES:(hd + 1) * LANES] = vseg.astype(BF16)

    ggq = ggq_ref[...]
    qg_scale = GQA_HEAD_DIM ** -0.5 * LOG2_E
    qg = proj(_C_QG, _C_KG)
    for hd in range(GQA_HEADS):
        seg = _rms(qg[:, hd * LANES:(hd + 1) * LANES], GQA_HEAD_DIM) * ggq
        seg = _rope(seg, cg, sag, sbg) * qg_scale
        qb_ref[:, hd * LANES:(hd + 1) * LANES] = seg.astype(BF16)
    ggk = ggk_ref[...]
    kvg = proj(_C_KG, _C_GA)
    kv_w = GQA_KV_HEADS * LANES
    for hd in range(GQA_KV_HEADS):
        seg = _rms(kvg[:, hd * LANES:(hd + 1) * LANES], GQA_HEAD_DIM) * ggk
        kb_ref[:, hd * LANES:(hd + 1) * LANES] = _rope(seg, cg, sag, sbg).astype(BF16)
        vseg = jnp.where(lane == GQA_ONES_LANE, 1.0, kvg[:, kv_w + hd * LANES:kv_w + (hd + 1) * LANES])
        vb_ref[:, hd * LANES:(hd + 1) * LANES] = vseg.astype(BF16)

    gates = jax.nn.sigmoid(proj(_C_GA, _C_END) + bgate_ref[...])
    ga_ref[...] = gates[:, :D_MODEL].astype(BF16)
    gb_ref[...] = gates[:, D_MODEL:].astype(BF16)


def _pre_call(x2d, seq, gattn, win, gql, wqup, gkvl, wkvup, gq, gk, ggq, ggk, bgate, tabs):
    t = x2d.shape[0]
    tm = PRE_TM
    n_seq_tiles = seq // tm

    def full(a):
        return pl.BlockSpec(a.shape, lambda i: (0,) * a.ndim)

    def tok(width):
        return pl.BlockSpec((tm, width), lambda i: (i, 0))

    tab_spec = pl.BlockSpec((tm, LANES), lambda i: (i % n_seq_tiles, 0))
    params = (gattn, win, gql, wqup, gkvl, wkvup, gq, gk, ggq, ggk, bgate)
    kv_w = GQA_KV_HEADS * LANES
    out_widths = (HEADS_WIDTH, HEADS_WIDTH, HEADS_WIDTH, HEADS_WIDTH, kv_w, kv_w, D_MODEL, D_MODEL)
    return pl.pallas_call(
        _pre_kernel,
        grid=(t // tm,),
        in_specs=[tok(D_MODEL)] + [full(a) for a in params] + [tab_spec] * 6,
        out_specs=[tok(w) for w in out_widths],
        out_shape=[jax.ShapeDtypeStruct((t, w), BF16) for w in out_widths],
        compiler_params=pltpu.CompilerParams(
            dimension_semantics=("arbitrary",), vmem_limit_bytes=VMEM_LIMIT),
        name="pre_attention",
    )(x2d, *params, *tabs)


def _flash_kernel(q_ref, k_ref, v_ref, o_ref, s0_ref, s1_ref, *, units, ones_lane):
    tq = s0_ref.shape[0]
    s_len = k_ref.shape[1]
    s_bufs = (s0_ref, s1_ref)

    def q_of(u):
        r, g = units[u]
        return q_ref[0, r * tq:(r + 1) * tq, g * LANES:(g + 1) * LANES]

    def score_pass(u, j, q, m128):
        cols = slice(j * ATT_TK, (j + 1) * ATT_TK)
        s = lax.dot_general(q, k_ref[0, cols, :], _NT, preferred_element_type=F32)
        s_bufs[u % 2][:, cols] = s
        for c in range(ATT_TK // LANES):
            m128 = jnp.maximum(m128, s[:, c * LANES:(c + 1) * LANES])
        return m128

    def value_pass(u, j, m, acc):
        cols = slice(j * ATT_TK2, (j + 1) * ATT_TK2)
        p = jnp.exp2(s_bufs[u % 2][:, cols] - m)
        return acc + _dot(p.astype(BF16), v_ref[0, cols, :])

    neg = jnp.full((tq, LANES), -1e30, F32)
    m128 = neg
    for j in range(s_len // ATT_TK):
        m128 = score_pass(0, j, q_of(0), m128)
    ratio = ATT_TK2 // ATT_TK
    for u in range(len(units)):
        m = jnp.max(m128, axis=-1, keepdims=True)
        acc = jnp.zeros((tq, LANES), F32)
        has_next = u + 1 < len(units)
        if has_next:
            q_next = q_of(u + 1)
            m128 = neg
        for j2 in range(s_len // ATT_TK2):
            acc = value_pass(u, j2, m, acc)
            if has_next:
                for j in range(j2 * ratio, (j2 + 1) * ratio):
                    m128 = score_pass(u + 1, j, q_next, m128)
        row_sum = acc[:, ones_lane:ones_lane + 1]
        r, g = units[u]
        o_ref[0, r * tq:(r + 1) * tq, g * LANES:(g + 1) * LANES] = (acc / row_sum).astype(BF16)


def _flash_call(q, k, v, group, ones_lane, name):
    b, s, qw = q.shape
    n_kv = k.shape[2] // LANES
    assert qw == n_kv * group * LANES and ATT_UNITS % group == 0
    assert s % ATT_TK2 == 0 and ATT_TK2 % ATT_TK == 0
    row_tiles = ATT_UNITS // group
    rows = ATT_TQ * row_tiles
    assert s % rows == 0
    units = tuple((r, g) for r in range(row_tiles) for g in range(group))
    q_spec = pl.BlockSpec((1, rows, group * LANES), lambda bi, kh, qi: (bi, qi, kh))
    kv_spec = pl.BlockSpec((1, s, LANES), lambda bi, kh, qi: (bi, 0, kh))
    return pl.pallas_call(
        functools.partial(_flash_kernel, units=units, ones_lane=ones_lane),
        grid=(b, n_kv, s // rows),
        in_specs=[q_spec, kv_spec, kv_spec],
        out_specs=q_spec,
        out_shape=jax.ShapeDtypeStruct(q.shape, BF16),
        scratch_shapes=[pltpu.VMEM((ATT_TQ, s), F32), pltpu.VMEM((ATT_TQ, s), F32)],
        compiler_params=pltpu.CompilerParams(
            dimension_semantics=("arbitrary", "arbitrary", "arbitrary"),
            vmem_limit_bytes=VMEM_LIMIT),
        name=name,
    )(q, k, v)


def _post_kernel(x_ref, ya_ref, yb_ref, ga_ref, gb_ref, wa_ref, wb_ref, wout_ref,
                 gffn_ref, wrh_ref, wrl_ref, br_ref, x1_ref, h2_ref, aff_ref):
    a = _dot(ya_ref[...], wa_ref[...])
    b = _dot(yb_ref[...], wb_ref[...])
    merged = ga_ref[...].astype(F32) * a + gb_ref[...].astype(F32) * b
    x1 = x_ref[...] + _dot(merged.astype(BF16), wout_ref[...])
    x1_ref[...] = x1
    h2 = _rms(x1, D_MODEL) * gffn_ref[...]
    h2_hi = h2.astype(BF16)
    h2_ref[...] = h2_hi
    h2_lo = (h2 - h2_hi.astype(F32)).astype(BF16)
    wrh = wrh_ref[...]
    logits = _dot(h2_hi, wrh) + _dot(h2_lo, wrh) + _dot(h2_hi, wrl_ref[...]) + br_ref[...]
    e = jnp.exp(logits - jnp.max(logits, axis=-1, keepdims=True))
    aff_ref[...] = e / jnp.sum(e, axis=-1, keepdims=True)


def _post_call(x2d, ya, yb, ga, gb, wa, wb, wout, gffn, wrh, wrl, br):
    t = x2d.shape[0]
    tm = PRE_TM

    def full(a):
        return pl.BlockSpec(a.shape, lambda i: (0,) * a.ndim)

    def tok(width):
        return pl.BlockSpec((tm, width), lambda i: (i, 0))

    params = (wa, wb, wout, gffn, wrh, wrl, br)
    return pl.pallas_call(
        _post_kernel,
        grid=(t // tm,),
        in_specs=[tok(D_MODEL)] * 5 + [full(a) for a in params],
        out_specs=[tok(D_MODEL), tok(D_MODEL), tok(LANES)],
        out_shape=[jax.ShapeDtypeStruct((t, D_MODEL), F32),
                   jax.ShapeDtypeStruct((t, D_MODEL), BF16),
                   jax.ShapeDtypeStruct((t, LANES), F32)],
        compiler_params=pltpu.CompilerParams(
            dimension_semantics=("arbitrary",), vmem_limit_bytes=VMEM_LIMIT),
        name="post_attention",
    )(x2d, ya, yb, ga, gb, *params)


def _window_meta(start, end, cap):
    lane = lax.broadcasted_iota(I32, start.shape, 1)
    first = jnp.minimum(jnp.floor(start * (1.0 / SLOT_ALIGN)) * SLOT_ALIGN, float(cap - MOE_WIN))
    over = jnp.where((lane < N_EXPERTS) & (end - first > MOE_WIN), 1.0, 0.0)
    flag = jnp.max(over, axis=-1, keepdims=True)
    return jnp.where(lane == META_FLAG_LANE, flag, first).astype(I32)


def _route_kernel(aff_ref, slot_t_ref, gate_t_ref, slot_r_ref, meta_ref, *, cap):
    s = aff_ref.shape[1]

    def count(mask):
        return jnp.sum(jnp.where(mask, 1.0, 0.0), axis=0, keepdims=True)

    def as_float(bits):
        return lax.bitcast_convert_type(bits, F32)

    def search(i, bits):
        cand = bits | jnp.left_shift(jnp.int32(1), 29 - i)
        return jnp.where(count(aff_ref[0] >= as_float(cand)) >= cap, cand, bits)

    bits = lax.fori_loop(0, 30, search, jnp.zeros((1, LANES), I32))
    is_normal = bits >= MIN_NORMAL_BITS
    thr = jnp.where(is_normal, as_float(bits), 0.0)
    thr_next = as_float(jnp.where(is_normal, bits + 1, MIN_NORMAL_BITS))
    need = cap - count(aff_ref[0] >= thr_next)

    blk = ROUTE_BLK
    tri = (lax.broadcasted_iota(I32, (blk, blk), 0) >= lax.broadcasted_iota(I32, (blk, blk), 1)).astype(BF16)
    eye = (lax.broadcasted_iota(I32, (LANES, LANES), 0) == lax.broadcasted_iota(I32, (LANES, LANES), 1)).astype(BF16)
    carry_eq = jnp.zeros((1, LANES), F32)
    carry_sel = jnp.zeros((1, LANES), F32)
    for c in range(s // blk):
        rows = slice(c * blk, (c + 1) * blk)
        ab = aff_ref[0, rows, :]
        gt = ab >= thr_next
        eq = jnp.where((ab >= thr) & jnp.logical_not(gt), 1.0, 0.0)
        eq_incl = _dot(tri, eq.astype(BF16)) + carry_eq
        carry_eq = eq_incl[blk - 1:blk, :]
        sel = jnp.where(gt | ((eq > 0.0) & (eq_incl - eq < need)), 1.0, 0.0)
        sel_incl = _dot(tri, sel.astype(BF16)) + carry_sel
        edge = carry_sel
        for k in range(blk // MOE_TC):
            nxt = sel_incl[(k + 1) * MOE_TC - 1:(k + 1) * MOE_TC, :]
            chunk = c * (blk // MOE_TC) + k
            meta_ref[0, chunk:chunk + 1, :] = _window_meta(edge, nxt, cap)
            edge = nxt
        carry_sel = sel_incl[blk - 1:blk, :]
        slot1 = sel * sel_incl
        slot_t_ref[0, rows, :] = slot1.astype(I32) - 1
        gate_t_ref[0, rows, :] = sel * aff_ref[0, rows, :]
        hi = jnp.floor(slot1 * (1.0 / 32.0))
        lo = slot1 - 32.0 * hi
        rows_hi = lax.dot_general(eye, hi.astype(BF16), _NT, preferred_element_type=F32)
        rows_lo = lax.dot_general(eye, lo.astype(BF16), _NT, preferred_element_type=F32)
        slot_rows = (32.0 * rows_hi + rows_lo).astype(I32) - 1
        slot_r_ref[0, :, rows] = slot_rows[:N_EXPERTS]


def _route_call(aff, cap):
    b, s, _ = aff.shape
    tok_spec = pl.BlockSpec((1, s, LANES), lambda bi: (bi, 0, 0))
    return pl.pallas_call(
        functools.partial(_route_kernel, cap=cap),
        grid=(b,),
        in_specs=[tok_spec],
        out_specs=[tok_spec, tok_spec, pl.BlockSpec((1, N_EXPERTS, s), lambda bi: (bi, 0, 0)),
                   pl.BlockSpec((1, s // MOE_TC, LANES), lambda bi: (bi, 0, 0))],
        out_shape=[jax.ShapeDtypeStruct((b, s, LANES), I32),
                   jax.ShapeDtypeStruct((b, s, LANES), F32),
                   jax.ShapeDtypeStruct((b, N_EXPERTS, s), I32),
                   jax.ShapeDtypeStruct((b, s // MOE_TC, LANES), I32)],
        compiler_params=pltpu.CompilerParams(
            dimension_semantics=("arbitrary",), vmem_limit_bytes=VMEM_LIMIT),
        name="route",
    )(aff)


def _meta_spec():
    return pl.BlockSpec((1, 1, 1, LANES), lambda bi, ci: (bi, ci, 0, 0), memory_space=pltpu.SMEM)


def _dispatch_kernel(meta_ref, slot_r_ref, h2_ref, xe_ref, *, cap):
    tc = h2_ref.shape[1]

    @pl.when(pl.program_id(1) == 0)
    def _():
        xe_ref[...] = jnp.zeros_like(xe_ref)

    slot_rows = slot_r_ref[0]
    h2c = h2_ref[0]

    def add_rows(first_row, rows):
        n = rows.shape[0]
        cur = xe_ref[0, pl.ds(first_row, n), :].astype(F32)
        xe_ref[0, pl.ds(first_row, n), :] = (cur + rows).astype(BF16)

    fits = meta_ref[0, 0, 0,META_FLAG_LANE] == 0

    @pl.when(fits)
    def _():
        firsts = [pl.multiple_of(meta_ref[0, 0, 0,e], SLOT_ALIGN) for e in range(N_EXPERTS)]
        win_ids = lax.broadcasted_iota(I32, (MOE_WIN, tc), 0)
        half = N_EXPERTS // 2
        for lo in (0, half):
            pick = jnp.concatenate(
                [jnp.where(win_ids == slot_rows[e:e + 1, :] - firsts[e], 1.0, 0.0).astype(BF16)
                 for e in range(lo, lo + half)], axis=0)
            got = _dot(pick, h2c)
            for i, e in enumerate(range(lo, lo + half)):
                add_rows(e * cap + firsts[e], got[i * MOE_WIN:(i + 1) * MOE_WIN])

    @pl.when(jnp.logical_not(fits))
    def _():
        slot_ids = lax.broadcasted_iota(I32, (cap, tc), 0)
        for e in range(N_EXPERTS):
            pick = jnp.where(slot_ids == slot_rows[e:e + 1, :], 1.0, 0.0).astype(BF16)
            add_rows(e * cap, _dot(pick, h2c))


def _dispatch_call(meta, slot_r, h2, cap):
    b, s, d = h2.shape
    tc = MOE_TC
    return pl.pallas_call(
        functools.partial(_dispatch_kernel, cap=cap),
        grid=(b, s // tc),
        in_specs=[_meta_spec(),
                  pl.BlockSpec((1, N_EXPERTS, tc), lambda bi, ci: (bi, 0, ci)),
                  pl.BlockSpec((1, tc, d), lambda bi, ci: (bi, ci, 0))],
        out_specs=pl.BlockSpec((1, N_EXPERTS * cap, d), lambda bi, ci: (bi, 0, 0)),
        out_shape=jax.ShapeDtypeStruct((b, N_EXPERTS * cap, d), BF16),
        compiler_params=pltpu.CompilerParams(
            dimension_semantics=("arbitrary", "arbitrary"), vmem_limit_bytes=VMEM_LIMIT),
        name="dispatch",
    )(meta, slot_r, h2)


def _expert_kernel(xe_ref, wg_ref, wu_ref, wd_ref, y_ref):
    xe = xe_ref[0]
    a = _dot(xe, wg_ref[0])
    u = _dot(xe, wu_ref[0])
    act = (a * jax.nn.sigmoid(a) * u).astype(BF16)
    y_ref[0] = _dot(act, wd_ref[0]).astype(BF16)


def _expert_call(xe, wg, wu, wd, cap):
    b, _, d = xe.shape
    w_spec = pl.BlockSpec((1, d, EXPERT_FF), lambda bi, e: (e, 0, 0))
    tok_spec = pl.BlockSpec((1, cap, d), lambda bi, e: (bi, e, 0))
    return pl.pallas_call(
        _expert_kernel,
        grid=(b, N_EXPERTS),
        in_specs=[tok_spec, w_spec, w_spec, pl.BlockSpec((1, EXPERT_FF, d), lambda bi, e: (e, 0, 0))],
        out_specs=tok_spec,
        out_shape=jax.ShapeDtypeStruct(xe.shape, BF16),
        compiler_params=pltpu.CompilerParams(
            dimension_semantics=("arbitrary", "arbitrary"), vmem_limit_bytes=VMEM_LIMIT),
        name="experts",
    )(xe, wg, wu, wd)


def _combine_kernel(meta_ref, x1_ref, slot_t_ref, gate_t_ref, y_ref, o_ref, ywin_ref, *, cap):
    tc = x1_ref.shape[1]
    slot_t = slot_t_ref[0]
    gate_t = gate_t_ref[0]
    fits = meta_ref[0, 0, 0,META_FLAG_LANE] == 0

    @pl.when(fits)
    def _():
        win_ids = lax.broadcasted_iota(I32, (tc, MOE_WIN), 1)
        pieces = []
        for e in range(N_EXPERTS):
            first = pl.multiple_of(meta_ref[0, 0, 0,e], SLOT_ALIGN)
            ywin_ref[e * MOE_WIN:(e + 1) * MOE_WIN, :] = y_ref[0, pl.ds(e * cap + first, MOE_WIN), :]
            hit = win_ids == slot_t[:, e:e + 1] - first
            pieces.append(jnp.where(hit, gate_t[:, e:e + 1], 0.0).astype(BF16))
        o_ref[0] = x1_ref[0] + _dot(jnp.concatenate(pieces, axis=1), ywin_ref[...])

    @pl.when(jnp.logical_not(fits))
    def _():
        acc = x1_ref[0]
        slot_ids = lax.broadcasted_iota(I32, (tc, cap), 1)
        for e in range(N_EXPERTS):
            place = jnp.where(slot_ids == slot_t[:, e:e + 1], gate_t[:, e:e + 1], 0.0).astype(BF16)
            acc = acc + _dot(place, y_ref[0, e * cap:(e + 1) * cap, :])
        o_ref[0] = acc


def _combine_call(meta, x1, slot_t, gate_t, y, cap):
    b, s, d = x1.shape
    tc = MOE_TC
    return pl.pallas_call(
        functools.partial(_combine_kernel, cap=cap),
        grid=(b, s // tc),
        in_specs=[_meta_spec(),
                  pl.BlockSpec((1, tc, d), lambda bi, ci: (bi, ci, 0)),
                  pl.BlockSpec((1, tc, LANES), lambda bi, ci: (bi, ci, 0)),
                  pl.BlockSpec((1, tc, LANES), lambda bi, ci: (bi, ci, 0)),
                  pl.BlockSpec((1, N_EXPERTS * cap, d), lambda bi, ci: (bi, 0, 0))],
        out_specs=pl.BlockSpec((1, tc, d), lambda bi, ci: (bi, ci, 0)),
        out_shape=jax.ShapeDtypeStruct((b, s, d), F32),
        scratch_shapes=[pltpu.VMEM((N_EXPERTS * MOE_WIN, d), BF16)],
        compiler_params=pltpu.CompilerParams(
            dimension_semantics=("arbitrary", "arbitrary"), vmem_limit_bytes=VMEM_LIMIT),
        name="combine",
    )(meta, x1, slot_t, gate_t, y)


def _rope_tables(n, rot_dim, lane_lo, lane_hi, period):
    rows = n // GRID_W
    row = jnp.broadcast_to(jnp.arange(rows)[:, None], (rows, GRID_W)).reshape(n).astype(F32)
    col = jnp.broadcast_to(jnp.arange(GRID_W)[None, :], (rows, GRID_W)).reshape(n).astype(F32)
    nf = rot_dim // 4
    inv = ROPE_THETA ** (-jnp.arange(nf, dtype=F32) / nf)
    ang = jnp.concatenate([row[:, None] * inv, col[:, None] * inv], axis=-1)
    cos, sin = jnp.cos(ang), jnp.sin(ang)
    lane = np.arange(LANES)
    active = (lane >= lane_lo) & (lane < lane_hi)
    pair = np.where(active, ((lane - lane_lo) % period) // 2, 0)
    even = (lane % 2) == 0
    c = jnp.where(active[None, :], cos[:, pair], 1.0)
    sa = jnp.where((active & even)[None, :], -sin[:, pair], 0.0)
    sb = jnp.where((active & ~even)[None, :], sin[:, pair], 0.0)
    return c, sa, sb


def _pad_cols(w, heads, width):
    k = w.shape[0]
    w = w.reshape(k, heads, width)
    return jnp.pad(w, ((0, 0), (0, 0), (0, LANES - width))).reshape(k, heads * LANES)


def kernel(x, g_attn_norm, w_in, b_gate, g_q_lat, w_q_up, g_kv_lat, w_kv_up, g_mla_qnorm, g_mla_knorm, g_gqa_qnorm, g_gqa_knorm, w_mla_branch, w_gqa_branch, w_out, g_ffn_norm, w_router, b_router, w_exp_gate, w_exp_up, w_exp_down):
    b, s, d = x.shape
    assert d == D_MODEL and s % max(PRE_TM, ROUTE_BLK) == 0 and ROUTE_BLK % MOE_TC == 0
    cap = CAPACITY_FACTOR * s // N_EXPERTS
    assert cap >= MOE_WIN and cap % SLOT_ALIGN == 0
    t = b * s

    splits = np.cumsum([Q_LORA, KV_LORA, MLA_ROPE, GQA_HEADS * GQA_HEAD_DIM,
                        GQA_KV_HEADS * GQA_HEAD_DIM, GQA_KV_HEADS * GQA_HEAD_DIM, D_MODEL])
    w_ql, w_kvl, w_kr, w_qg, w_kg, w_vg, w_ga, w_gb = jnp.split(w_in, splits, axis=1)
    w_kr = jnp.pad(w_kr, ((0, 0), (MLA_NOPE, LANES - MLA_QK)))
    w_qg = _pad_cols(w_qg, GQA_HEADS, GQA_HEAD_DIM)
    w_kg = _pad_cols(w_kg, GQA_KV_HEADS, GQA_HEAD_DIM)
    w_vg = _pad_cols(w_vg, GQA_KV_HEADS, GQA_HEAD_DIM)
    win = jnp.concatenate([w_ql, w_kvl, w_kr, w_qg, w_kg, w_vg, w_ga, w_gb], axis=1).astype(BF16)
    wqup = _pad_cols(w_q_up, MLA_HEADS, MLA_QK).astype(BF16)
    wkvup = w_kv_up.astype(BF16)

    def row(v):
        return v.reshape(1, -1).astype(F32)

    gq = row(jnp.pad(g_mla_qnorm, (0, LANES - MLA_QK)))
    gk = row(jnp.pad(g_mla_knorm, (0, LANES - MLA_QK)))
    ggq = row(jnp.pad(g_gqa_qnorm, (0, LANES - GQA_HEAD_DIM)))
    ggk = row(jnp.pad(g_gqa_knorm, (0, LANES - GQA_HEAD_DIM)))

    wa = jnp.pad(w_mla_branch.reshape(MLA_HEADS, MLA_V, d), ((0, 0), (LANES - MLA_V, 0), (0, 0)))
    wa = wa.reshape(HEADS_WIDTH, d).astype(BF16)
    wb = jnp.pad(w_gqa_branch.reshape(GQA_HEADS, GQA_HEAD_DIM, d), ((0, 0), (0, LANES - GQA_HEAD_DIM), (0, 0)))
    wb = wb.reshape(HEADS_WIDTH, d).astype(BF16)

    wr = jnp.pad(w_router, ((0, 0), (0, LANES - N_EXPERTS)))
    wr_hi = wr.astype(BF16)
    wr_lo = (wr - wr_hi.astype(F32)).astype(BF16)
    br = jnp.concatenate([b_router.astype(F32), jnp.full((LANES - N_EXPERTS,), -1e30, F32)]).reshape(1, LANES)

    tabs = (_rope_tables(s, MLA_ROPE, MLA_NOPE, MLA_QK, MLA_ROPE)
            + _rope_tables(s, GQA_HEAD_DIM, 0, GQA_HEAD_DIM, GQA_HEAD_DIM))

    x2d = x.reshape(t, d)
    qa, ka, va, qb, kb, vb, ga, gb = _pre_call(
        x2d, s, row(g_attn_norm), win, row(g_q_lat), wqup, row(g_kv_lat), wkvup,
        gq, gk, ggq, ggk, row(b_gate), tabs)

    def seq(a):
        return a.reshape(b, s, a.shape[-1])

    ya = _flash_call(seq(qa), seq(ka), seq(va), 1, MLA_ONES_LANE, "attention_mla")
    yb = _flash_call(seq(qb), seq(kb), seq(vb), GQA_GROUP, GQA_ONES_LANE, "attention_gqa")

    x1, h2, aff = _post_call(x2d, ya.reshape(t, HEADS_WIDTH), yb.reshape(t, HEADS_WIDTH), ga, gb,
                             wa, wb, w_out.astype(BF16), row(g_ffn_norm), wr_hi, wr_lo, br)

    slot_t, gate_t, slot_r, meta = _route_call(seq(aff), cap)
    meta = meta.reshape(b, s // MOE_TC, 1, LANES)
    xe = _dispatch_call(meta, slot_r, seq(h2), cap)
    y = _expert_call(xe, w_exp_gate.astype(BF16), w_exp_up.astype(BF16), w_exp_down.astype(BF16), cap)
    return _combine_call(meta, seq(x1), slot_t, gate_t, y, cap)
```

```python
import functools

import jax
import jax.numpy as jnp
import numpy as np
from jax import lax
from jax.experimental import pallas as pl
from jax.experimental.pallas import tpu as pltpu

F32 = jnp.float32
BF16 = jnp.bfloat16
I32 = jnp.int32

D_MODEL = 1024
GRID_W = 64
ROPE_THETA = 10000.0
EPS = 1e-6

MLA_HEADS = 8
MLA_NOPE = 64
MLA_ROPE = 32
MLA_QK = MLA_NOPE + MLA_ROPE
MLA_V = 64
Q_LORA = 768
KV_LORA = 256

GQA_HEADS = 8
GQA_KV_HEADS = 2
GQA_HEAD_DIM = 64
GQA_GROUP = GQA_HEADS // GQA_KV_HEADS

N_EXPERTS = 16
CAPACITY_FACTOR = 2
EXPERT_FF = 1024

MIN_NORMAL_BITS = 0x00800000
LANES = 128
HEADS_WIDTH = MLA_HEADS * LANES

_C_QLAT = 0
_C_KVLAT = _C_QLAT + Q_LORA
_C_KROPE = _C_KVLAT + KV_LORA
_C_QG = _C_KROPE + LANES
_C_KG = _C_QG + HEADS_WIDTH
_C_VG = _C_KG + GQA_KV_HEADS * LANES
_C_GA = _C_VG + GQA_KV_HEADS * LANES
_C_GB = _C_GA + D_MODEL
_C_END = _C_GB + D_MODEL

MLA_ONES_LANE = 0
GQA_ONES_LANE = GQA_HEAD_DIM
LOG2_E = 1.4426950408889634

PRE_TM = 256
ATT_TQ = 256
ATT_UNITS = 8
ATT_TK = 512
ATT_TK2 = 2048
ROUTE_BLK = 512
MOE_TC = 256
MOE_WIN = 128
SLOT_ALIGN = 16
META_FLAG_LANE = N_EXPERTS
VMEM_LIMIT = 56 * 1024 * 1024

_NT = (((1,), (1,)), ((), ()))


def _dot(a, b):
    return jnp.dot(a, b, preferred_element_type=F32)


def _rms(x, width):
    return x * lax.rsqrt(jnp.sum(x * x, axis=-1, keepdims=True) * (1.0 / width) + EPS)


def _rope(x, c, sa, sb):
    return x * c + pltpu.roll(x, LANES - 1, 1) * sa + pltpu.roll(x, 1, 1) * sb


def _pre_kernel(x_ref, gattn_ref, win_ref, gql_ref, wqup_ref, gkvl_ref, wkvup_ref,
                gk_ref, ggk_ref, bgate_ref,
                cm_ref, sam_ref, sbm_ref, cg_ref, sag_ref, sbg_ref,
                qa_ref, ka_ref, va_ref, qb_ref, kb_ref, vb_ref, ga_ref, gb_ref):
    x = x_ref[...]
    h = (_rms(x, D_MODEL) * gattn_ref[...]).astype(BF16)

    def proj(lo, hi):
        return _dot(h, win_ref[:, lo:hi])

    lane = lax.broadcasted_iota(I32, (x.shape[0], LANES), 1)
    low_half = lane < GQA_HEAD_DIM
    cm, sam, sbm = cm_ref[...], sam_ref[...], sbm_ref[...]
    cg, sag, sbg = cg_ref[...], sag_ref[...], sbg_ref[...]

    latents = proj(_C_QLAT, _C_QG)
    c_q = (_rms(latents[:, _C_QLAT:_C_KVLAT], Q_LORA) * gql_ref[...]).astype(BF16)
    qa_ref[...] = _dot(c_q, wqup_ref[...])

    c_kv = (_rms(latents[:, _C_KVLAT:_C_KROPE], KV_LORA) * gkvl_ref[...]).astype(BF16)
    k_rope = latents[:, _C_KROPE:_C_QG]
    kv_up = _dot(c_kv, wkvup_ref[...])
    gk = gk_ref[...]
    for hd in range(MLA_HEADS):
        kv = kv_up[:, hd * LANES:(hd + 1) * LANES]
        kseg = jnp.where(low_half, kv, 0.0) + k_rope
        kseg = _rms(kseg, MLA_QK) * gk
        ka_ref[:, hd * LANES:(hd + 1) * LANES] = _rope(kseg, cm, sam, sbm).astype(BF16)
        vseg = jnp.where(lane == MLA_ONES_LANE, 1.0, jnp.where(low_half, 0.0, kv))
        va_ref[:, hd * LANES:(hd + 1) * LANES] = vseg.astype(BF16)

    qb_ref[...] = proj(_C_QG, _C_KG)
    ggk = ggk_ref[...]
    kvg = proj(_C_KG, _C_GA)
    kv_w = GQA_KV_HEADS * LANES
    for hd in range(GQA_KV_HEADS):
        seg = _rms(kvg[:, hd * LANES:(hd + 1) * LANES], GQA_HEAD_DIM) * ggk
        kb_ref[:, hd * LANES:(hd + 1) * LANES] = _rope(seg, cg, sag, sbg).astype(BF16)
        vseg = jnp.where(lane == GQA_ONES_LANE, 1.0, kvg[:, kv_w + hd * LANES:kv_w + (hd + 1) * LANES])
        vb_ref[:, hd * LANES:(hd + 1) * LANES] = vseg.astype(BF16)

    gates = jax.nn.sigmoid(proj(_C_GA, _C_END) + bgate_ref[...])
    ga_ref[...] = gates[:, :D_MODEL].astype(BF16)
    gb_ref[...] = gates[:, D_MODEL:].astype(BF16)


def _pre_call(x2d, seq, gattn, win, gql, wqup, gkvl, wkvup, gk, ggk, bgate, tabs):
    t = x2d.shape[0]
    tm = PRE_TM
    n_seq_tiles = seq // tm

    def full(a):
        return pl.BlockSpec(a.shape, lambda i: (0,) * a.ndim)

    def tok(width):
        return pl.BlockSpec((tm, width), lambda i: (i, 0))

    tab_spec = pl.BlockSpec((tm, LANES), lambda i: (i % n_seq_tiles, 0))
    params = (gattn, win, gql, wqup, gkvl, wkvup, gk, ggk, bgate)
    kv_w = GQA_KV_HEADS * LANES
    outs = ((HEADS_WIDTH, F32), (HEADS_WIDTH, BF16), (HEADS_WIDTH, BF16), (HEADS_WIDTH, F32),
            (kv_w, BF16), (kv_w, BF16), (D_MODEL, BF16), (D_MODEL, BF16))
    return pl.pallas_call(
        _pre_kernel,
        grid=(t // tm,),
        in_specs=[tok(D_MODEL)] + [full(a) for a in params] + [tab_spec] * 6,
        out_specs=[tok(w) for w, _ in outs],
        out_shape=[jax.ShapeDtypeStruct((t, w), dt) for w, dt in outs],
        compiler_params=pltpu.CompilerParams(
            dimension_semantics=("arbitrary",), vmem_limit_bytes=VMEM_LIMIT),
        name="pre_attention",
    )(x2d, *params, *tabs)


def _flash_kernel(q_ref, gain_ref, c_ref, sa_ref, sb_ref, k_ref, v_ref, o_ref, s0_ref, s1_ref, *,
                  units, ones_lane, head_dim):
    tq = s0_ref.shape[0]
    s_len = k_ref.shape[1]
    s_bufs = (s0_ref, s1_ref)
    q_scale = head_dim ** -0.5 * LOG2_E

    def q_of(u):
        r, g = units[u]
        rows = slice(r * tq, (r + 1) * tq)
        q = _rms(q_ref[0, rows, g * LANES:(g + 1) * LANES], head_dim) * gain_ref[...]
        q = _rope(q, c_ref[rows, :], sa_ref[rows, :], sb_ref[rows, :]) * q_scale
        return q.astype(BF16)

    def score_pass(u, j, q, m128):
        cols = slice(j * ATT_TK, (j + 1) * ATT_TK)
        s = lax.dot_general(q, k_ref[0, cols, :], _NT, preferred_element_type=F32)
        s_bufs[u % 2][:, cols] = s
        for c in range(ATT_TK // LANES):
            m128 = jnp.maximum(m128, s[:, c * LANES:(c + 1) * LANES])
        return m128

    def value_pass(u, j, m, acc):
        cols = slice(j * ATT_TK2, (j + 1) * ATT_TK2)
        p = jnp.exp2(s_bufs[u % 2][:, cols] - m)
        return acc + _dot(p.astype(BF16), v_ref[0, cols, :])

    neg = jnp.full((tq, LANES), -1e30, F32)
    m128 = neg
    q_first = q_of(0)
    for j in range(s_len // ATT_TK):
        m128 = score_pass(0, j, q_first, m128)
    ratio = ATT_TK2 // ATT_TK
    for u in range(len(units)):
        m = jnp.max(m128, axis=-1, keepdims=True)
        acc = jnp.zeros((tq, LANES), F32)
        has_next = u + 1 < len(units)
        if has_next:
            q_next = q_of(u + 1)
            m128 = neg
        for j2 in range(s_len // ATT_TK2):
            acc = value_pass(u, j2, m, acc)
            if has_next:
                for j in range(j2 * ratio, (j2 + 1) * ratio):
                    m128 = score_pass(u + 1, j, q_next, m128)
        row_sum = acc[:, ones_lane:ones_lane + 1]
        r, g = units[u]
        o_ref[0, r * tq:(r + 1) * tq, g * LANES:(g + 1) * LANES] = (acc / row_sum).astype(BF16)


def _flash_call(q, gain, tabs, k, v, group, ones_lane, head_dim, name):
    b, s, qw = q.shape
    n_kv = k.shape[2] // LANES
    assert qw == n_kv * group * LANES and ATT_UNITS % group == 0
    assert s % ATT_TK2 == 0 and ATT_TK2 % ATT_TK == 0
    row_tiles = ATT_UNITS // group
    rows = ATT_TQ * row_tiles
    assert s % rows == 0
    units = tuple((r, g) for r in range(row_tiles) for g in range(group))
    q_spec = pl.BlockSpec((1, rows, group * LANES), lambda bi, kh, qi: (bi, qi, kh))
    kv_spec = pl.BlockSpec((1, s, LANES), lambda bi, kh, qi: (bi, 0, kh))
    tab_spec = pl.BlockSpec((rows, LANES), lambda bi, kh, qi: (qi, 0))
    gain_spec = pl.BlockSpec((1, LANES), lambda bi, kh, qi: (0, 0))
    return pl.pallas_call(
        functools.partial(_flash_kernel, units=units, ones_lane=ones_lane, head_dim=head_dim),
        grid=(b, n_kv, s // rows),
        in_specs=[q_spec, gain_spec, tab_spec, tab_spec, tab_spec, kv_spec, kv_spec],
        out_specs=q_spec,
        out_shape=jax.ShapeDtypeStruct(q.shape, BF16),
        scratch_shapes=[pltpu.VMEM((ATT_TQ, s), F32), pltpu.VMEM((ATT_TQ, s), F32)],
        compiler_params=pltpu.CompilerParams(
            dimension_semantics=("arbitrary", "arbitrary", "arbitrary"),
            vmem_limit_bytes=VMEM_LIMIT),
        name=name,
    )(q, gain, *tabs, k, v)


def _post_kernel(x_ref, ya_ref, yb_ref, ga_ref, gb_ref, wa_ref, wb_ref, wout_ref,
                 gffn_ref, wrh_ref, wrl_ref, br_ref, x1_ref, h2_ref, aff_ref, aff_rows_ref):
    a = _dot(ya_ref[...], wa_ref[...])
    b = _dot(yb_ref[...], wb_ref[...])
    merged = ga_ref[...].astype(F32) * a + gb_ref[...].astype(F32) * b
    x1 = x_ref[...] + _dot(merged.astype(BF16), wout_ref[...])
    x1_ref[...] = x1
    h2 = _rms(x1, D_MODEL) * gffn_ref[...]
    h2_hi = h2.astype(BF16)
    h2_ref[...] = h2_hi
    h2_lo = (h2 - h2_hi.astype(F32)).astype(BF16)
    wrh = wrh_ref[...]
    logits = _dot(h2_hi, wrh) + _dot(h2_lo, wrh) + _dot(h2_hi, wrl_ref[...]) + br_ref[...]
    e = jnp.exp(logits - jnp.max(logits, axis=-1, keepdims=True))
    aff = e / jnp.sum(e, axis=-1, keepdims=True)
    aff_ref[...] = aff
    aff_rows_ref[...] = jnp.transpose(aff)[:N_EXPERTS]


def _post_call(x2d, ya, yb, ga, gb, wa, wb, wout, gffn, wrh, wrl, br):
    t = x2d.shape[0]
    tm = PRE_TM

    def full(a):
        return pl.BlockSpec(a.shape, lambda i: (0,) * a.ndim)

    def tok(width):
        return pl.BlockSpec((tm, width), lambda i: (i, 0))

    params = (wa, wb, wout, gffn, wrh, wrl, br)
    return pl.pallas_call(
        _post_kernel,
        grid=(t // tm,),
        in_specs=[tok(D_MODEL)] * 5 + [full(a) for a in params],
        out_specs=[tok(D_MODEL), tok(D_MODEL), tok(LANES),
                   pl.BlockSpec((N_EXPERTS, tm), lambda i: (0, i))],
        out_shape=[jax.ShapeDtypeStruct((t, D_MODEL), F32),
                   jax.ShapeDtypeStruct((t, D_MODEL), BF16),
                   jax.ShapeDtypeStruct((t, LANES), F32),
                   jax.ShapeDtypeStruct((N_EXPERTS, t), F32)],
        compiler_params=pltpu.CompilerParams(
            dimension_semantics=("arbitrary",), vmem_limit_bytes=VMEM_LIMIT),
        name="post_attention",
    )(x2d, ya, yb, ga, gb, *params)


def _window_meta(start, end, cap):
    lane = lax.broadcasted_iota(I32, start.shape, 1)
    first = jnp.minimum(jnp.floor(start * (1.0 / SLOT_ALIGN)) * SLOT_ALIGN, float(cap - MOE_WIN))
    over = jnp.where((lane < N_EXPERTS) & (end - first > MOE_WIN), 1.0, 0.0)
    flag = jnp.max(over, axis=-1, keepdims=True)
    return jnp.where(lane == META_FLAG_LANE, flag, first).astype(I32)


def _route_kernel(aff_ref, aff_rows_ref, slot_t_ref, gate_t_ref, slot_r_ref, meta_ref, *, cap):
    s = aff_ref.shape[1]

    def count(mask):
        return jnp.sum(jnp.where(mask, 1.0, 0.0), axis=1, keepdims=True)

    def as_float(bits):
        return lax.bitcast_convert_type(bits, F32)

    def search(i, bits):
        cand = bits | jnp.left_shift(jnp.int32(1), 29 - i)
        return jnp.where(count(aff_rows_ref[...] >= as_float(cand)) >= cap, cand, bits)

    bits = lax.fori_loop(0, 30, search, jnp.zeros((N_EXPERTS, 1), I32))
    is_normal = bits >= MIN_NORMAL_BITS
    thr_col = jnp.where(is_normal, as_float(bits), 0.0)
    next_col = as_float(jnp.where(is_normal, bits + 1, MIN_NORMAL_BITS))
    need_col = cap - count(aff_rows_ref[...] >= next_col)

    def to_lanes(col):
        on_diag = (lax.broadcasted_iota(I32, (N_EXPERTS, LANES), 0)
                   == lax.broadcasted_iota(I32, (N_EXPERTS, LANES), 1))
        return jnp.sum(jnp.where(on_diag, col, 0.0), axis=0, keepdims=True)

    thr, thr_next, need = to_lanes(thr_col), to_lanes(next_col), to_lanes(need_col)

    blk = ROUTE_BLK
    tri = (lax.broadcasted_iota(I32, (blk, blk), 0) >= lax.broadcasted_iota(I32, (blk, blk), 1)).astype(BF16)
    eye = (lax.broadcasted_iota(I32, (LANES, LANES), 0) == lax.broadcasted_iota(I32, (LANES, LANES), 1)).astype(BF16)
    carry_eq = jnp.zeros((1, LANES), F32)
    carry_sel = jnp.zeros((1, LANES), F32)
    for c in range(s // blk):
        rows = slice(c * blk, (c + 1) * blk)
        ab = aff_ref[0, rows, :]
        gt = ab >= thr_next
        eq = jnp.where((ab >= thr) & jnp.logical_not(gt), 1.0, 0.0)
        eq_incl = _dot(tri, eq.astype(BF16)) + carry_eq
        carry_eq = eq_incl[blk - 1:blk, :]
        sel = jnp.where(gt | ((eq > 0.0) & (eq_incl - eq < need)), 1.0, 0.0)
        sel_incl = _dot(tri, sel.astype(BF16)) + carry_sel
        edge = carry_sel
        for k in range(blk // MOE_TC):
            nxt = sel_incl[(k + 1) * MOE_TC - 1:(k + 1) * MOE_TC, :]
            chunk = c * (blk // MOE_TC) + k
            meta_ref[0, chunk:chunk + 1, :] = _window_meta(edge, nxt, cap)
            edge = nxt
        carry_sel = sel_incl[blk - 1:blk, :]
        slot1 = sel * sel_incl
        slot_t_ref[0, rows, :] = slot1.astype(I32) - 1
        gate_t_ref[0, rows, :] = sel * aff_ref[0, rows, :]
        hi = jnp.floor(slot1 * (1.0 / 32.0))
        lo = slot1 - 32.0 * hi
        rows_hi = lax.dot_general(eye, hi.astype(BF16), _NT, preferred_element_type=F32)
        rows_lo = lax.dot_general(eye, lo.astype(BF16), _NT, preferred_element_type=F32)
        slot_rows = (32.0 * rows_hi + rows_lo).astype(I32) - 1
        slot_r_ref[0, :, rows] = slot_rows[:N_EXPERTS]


def _route_call(aff, aff_rows, cap):
    b, s, _ = aff.shape
    tok_spec = pl.BlockSpec((1, s, LANES), lambda bi: (bi, 0, 0))
    return pl.pallas_call(
        functools.partial(_route_kernel, cap=cap),
        grid=(b,),
        in_specs=[tok_spec, pl.BlockSpec((N_EXPERTS, s), lambda bi: (0, bi))],
        out_specs=[tok_spec, tok_spec, pl.BlockSpec((1, N_EXPERTS, s), lambda bi: (bi, 0, 0)),
                   pl.BlockSpec((1, s // MOE_TC, LANES), lambda bi: (bi, 0, 0))],
        out_shape=[jax.ShapeDtypeStruct((b, s, LANES), I32),
                   jax.ShapeDtypeStruct((b, s, LANES), F32),
                   jax.ShapeDtypeStruct((b, N_EXPERTS, s), I32),
                   jax.ShapeDtypeStruct((b, s // MOE_TC, LANES), I32)],
        compiler_params=pltpu.CompilerParams(
            dimension_semantics=("arbitrary",), vmem_limit_bytes=VMEM_LIMIT),
        name="route",
    )(aff, aff_rows)


def _meta_spec():
    return pl.BlockSpec((1, 1, 1, LANES), lambda bi, ci: (bi, ci, 0, 0), memory_space=pltpu.SMEM)


def _dispatch_kernel(meta_ref, slot_r_ref, h2_ref, xe_ref, *, cap):
    tc = h2_ref.shape[1]

    @pl.when(pl.program_id(1) == 0)
    def _():
        xe_ref[...] = jnp.zeros_like(xe_ref)

    slot_rows = slot_r_ref[0]
    h2c = h2_ref[0]

    def add_rows(first_row, rows):
        n = rows.shape[0]
        cur = xe_ref[0, pl.ds(first_row, n), :].astype(F32)
        xe_ref[0, pl.ds(first_row, n), :] = (cur + rows).astype(BF16)

    fits = meta_ref[0, 0, 0,META_FLAG_LANE] == 0

    @pl.when(fits)
    def _():
        firsts = [pl.multiple_of(meta_ref[0, 0, 0,e], SLOT_ALIGN) for e in range(N_EXPERTS)]
        win_ids = lax.broadcasted_iota(I32, (MOE_WIN, tc), 0)
        half = N_EXPERTS // 2
        for lo in (0, half):
            pick = jnp.concatenate(
                [jnp.where(win_ids == slot_rows[e:e + 1, :] - firsts[e], 1.0, 0.0).astype(BF16)
                 for e in range(lo, lo + half)], axis=0)
            got = _dot(pick, h2c)
            for i, e in enumerate(range(lo, lo + half)):
                add_rows(e * cap + firsts[e], got[i * MOE_WIN:(i + 1) * MOE_WIN])

    @pl.when(jnp.logical_not(fits))
    def _():
        slot_ids = lax.broadcasted_iota(I32, (cap, tc), 0)
        for e in range(N_EXPERTS):
            pick = jnp.where(slot_ids == slot_rows[e:e + 1, :], 1.0, 0.0).astype(BF16)
            add_rows(e * cap, _dot(pick, h2c))


def _dispatch_call(meta, slot_r, h2, cap):
    b, s, d = h2.shape
    tc = MOE_TC
    return pl.pallas_call(
        functools.partial(_dispatch_kernel, cap=cap),
        grid=(b, s // tc),
        in_specs=[_meta_spec(),
                  pl.BlockSpec((1, N_EXPERTS, tc), lambda bi, ci: (bi, 0, ci)),
                  pl.BlockSpec((1, tc, d), lambda bi, ci: (bi, ci, 0))],
        out_specs=pl.BlockSpec((1, N_EXPERTS * cap, d), lambda bi, ci: (bi, 0, 0)),
        out_shape=jax.ShapeDtypeStruct((b, N_EXPERTS * cap, d), BF16),
        compiler_params=pltpu.CompilerParams(
            dimension_semantics=("arbitrary", "arbitrary"), vmem_limit_bytes=VMEM_LIMIT),
        name="dispatch",
    )(meta, slot_r, h2)


def _expert_kernel(xe_ref, wg_ref, wu_ref, wd_ref, y_ref, wg_bf, wu_bf, wd_bf):
    @pl.when(pl.program_id(1) == 0)
    def _():
        wg_bf[...] = wg_ref[0].astype(BF16)
        wu_bf[...] = wu_ref[0].astype(BF16)
        wd_bf[...] = wd_ref[0].astype(BF16)

    xe = xe_ref[0]
    a = _dot(xe, wg_bf[...])
    u = _dot(xe, wu_bf[...])
    act = (a * jax.nn.sigmoid(a) * u).astype(BF16)
    y_ref[0] = _dot(act, wd_bf[...]).astype(BF16)


def _expert_call(xe, wg, wu, wd, cap):
    b, _, d = xe.shape
    up_spec = pl.BlockSpec((1, d, EXPERT_FF), lambda e, bi: (e, 0, 0))
    down_spec = pl.BlockSpec((1, EXPERT_FF, d), lambda e, bi: (e, 0, 0))
    tok_spec = pl.BlockSpec((1, cap, d), lambda e, bi: (bi, e, 0))
    return pl.pallas_call(
        _expert_kernel,
        grid=(N_EXPERTS, b),
        in_specs=[tok_spec, up_spec, up_spec, down_spec],
        out_specs=tok_spec,
        out_shape=jax.ShapeDtypeStruct(xe.shape, BF16),
        scratch_shapes=[pltpu.VMEM((d, EXPERT_FF), BF16), pltpu.VMEM((d, EXPERT_FF), BF16),
                        pltpu.VMEM((EXPERT_FF, d), BF16)],
        compiler_params=pltpu.CompilerParams(
            dimension_semantics=("arbitrary", "arbitrary"), vmem_limit_bytes=VMEM_LIMIT),
        name="experts",
    )(xe, wg, wu, wd)


def _combine_kernel(meta_ref, x1_ref, slot_t_ref, gate_t_ref, y_ref, o_ref, ywin_ref, *, cap):
    tc = x1_ref.shape[1]
    slot_t = slot_t_ref[0]
    gate_t = gate_t_ref[0]
    fits = meta_ref[0, 0, 0,META_FLAG_LANE] == 0

    @pl.when(fits)
    def _():
        win_ids = lax.broadcasted_iota(I32, (tc, MOE_WIN), 1)
        pieces = []
        for e in range(N_EXPERTS):
            first = pl.multiple_of(meta_ref[0, 0, 0,e], SLOT_ALIGN)
            ywin_ref[e * MOE_WIN:(e + 1) * MOE_WIN, :] = y_ref[0, pl.ds(e * cap + first, MOE_WIN), :]
            hit = win_ids == slot_t[:, e:e + 1] - first
            pieces.append(jnp.where(hit, gate_t[:, e:e + 1], 0.0).astype(BF16))
        o_ref[0] = x1_ref[0] + _dot(jnp.concatenate(pieces, axis=1), ywin_ref[...])

    @pl.when(jnp.logical_not(fits))
    def _():
        acc = x1_ref[0]
        slot_ids = lax.broadcasted_iota(I32, (tc, cap), 1)
        for e in range(N_EXPERTS):
            place = jnp.where(slot_ids == slot_t[:, e:e + 1], gate_t[:, e:e + 1], 0.0).astype(BF16)
            acc = acc + _dot(place, y_ref[0, e * cap:(e + 1) * cap, :])
        o_ref[0] = acc


def _combine_call(meta, x1, slot_t, gate_t, y, cap):
    b, s, d = x1.shape
    tc = MOE_TC
    return pl.pallas_call(
        functools.partial(_combine_kernel, cap=cap),
        grid=(b, s // tc),
        in_specs=[_meta_spec(),
                  pl.BlockSpec((1, tc, d), lambda bi, ci: (bi, ci, 0)),
                  pl.BlockSpec((1, tc, LANES), lambda bi, ci: (bi, ci, 0)),
                  pl.BlockSpec((1, tc, LANES), lambda bi, ci: (bi, ci, 0)),
                  pl.BlockSpec((1, N_EXPERTS * cap, d), lambda bi, ci: (bi, 0, 0))],
        out_specs=pl.BlockSpec((1, tc, d), lambda bi, ci: (bi, ci, 0)),
        out_shape=jax.ShapeDtypeStruct((b, s, d), F32),
        scratch_shapes=[pltpu.VMEM((N_EXPERTS * MOE_WIN, d), BF16)],
        compiler_params=pltpu.CompilerParams(
            dimension_semantics=("arbitrary", "arbitrary"), vmem_limit_bytes=VMEM_LIMIT),
        name="combine",
    )(meta, x1, slot_t, gate_t, y)


def _rope_tables(n, rot_dim, lane_lo, lane_hi, period):
    rows = n // GRID_W
    row = jnp.broadcast_to(jnp.arange(rows)[:, None], (rows, GRID_W)).reshape(n).astype(F32)
    col = jnp.broadcast_to(jnp.arange(GRID_W)[None, :], (rows, GRID_W)).reshape(n).astype(F32)
    nf = rot_dim // 4
    inv = ROPE_THETA ** (-jnp.arange(nf, dtype=F32) / nf)
    ang = jnp.concatenate([row[:, None] * inv, col[:, None] * inv], axis=-1)
    cos, sin = jnp.cos(ang), jnp.sin(ang)
    lane = np.arange(LANES)
    active = (lane >= lane_lo) & (lane < lane_hi)
    pair = np.where(active, ((lane - lane_lo) % period) // 2, 0)
    even = (lane % 2) == 0
    c = jnp.where(active[None, :], cos[:, pair], 1.0)
    sa = jnp.where((active & even)[None, :], -sin[:, pair], 0.0)
    sb = jnp.where((active & ~even)[None, :], sin[:, pair], 0.0)
    return c, sa, sb


def _pad_cols(w, heads, width):
    k = w.shape[0]
    w = w.reshape(k, heads, width)
    return jnp.pad(w, ((0, 0), (0, 0), (0, LANES - width))).reshape(k, heads * LANES)


def kernel(x, g_attn_norm, w_in, b_gate, g_q_lat, w_q_up, g_kv_lat, w_kv_up, g_mla_qnorm, g_mla_knorm, g_gqa_qnorm, g_gqa_knorm, w_mla_branch, w_gqa_branch, w_out, g_ffn_norm, w_router, b_router, w_exp_gate, w_exp_up, w_exp_down):
    b, s, d = x.shape
    assert d == D_MODEL and s % max(PRE_TM, ROUTE_BLK) == 0 and ROUTE_BLK % MOE_TC == 0
    cap = CAPACITY_FACTOR * s // N_EXPERTS
    assert cap >= MOE_WIN and cap % SLOT_ALIGN == 0
    t = b * s

    splits = np.cumsum([Q_LORA, KV_LORA, MLA_ROPE, GQA_HEADS * GQA_HEAD_DIM,
                        GQA_KV_HEADS * GQA_HEAD_DIM, GQA_KV_HEADS * GQA_HEAD_DIM, D_MODEL])
    w_ql, w_kvl, w_kr, w_qg, w_kg, w_vg, w_ga, w_gb = jnp.split(w_in, splits, axis=1)
    w_kr = jnp.pad(w_kr, ((0, 0), (MLA_NOPE, LANES - MLA_QK)))
    w_qg = _pad_cols(w_qg, GQA_HEADS, GQA_HEAD_DIM)
    w_kg = _pad_cols(w_kg, GQA_KV_HEADS, GQA_HEAD_DIM)
    w_vg = _pad_cols(w_vg, GQA_KV_HEADS, GQA_HEAD_DIM)
    win = jnp.concatenate([w_ql, w_kvl, w_kr, w_qg, w_kg, w_vg, w_ga, w_gb], axis=1).astype(BF16)
    wqup = _pad_cols(w_q_up, MLA_HEADS, MLA_QK).astype(BF16)
    wkvup = w_kv_up.astype(BF16)

    def row(v):
        return v.reshape(1, -1).astype(F32)

    gq = row(jnp.pad(g_mla_qnorm, (0, LANES - MLA_QK)))
    gk = row(jnp.pad(g_mla_knorm, (0, LANES - MLA_QK)))
    ggq = row(jnp.pad(g_gqa_qnorm, (0, LANES - GQA_HEAD_DIM)))
    ggk = row(jnp.pad(g_gqa_knorm, (0, LANES - GQA_HEAD_DIM)))

    wa = jnp.pad(w_mla_branch.reshape(MLA_HEADS, MLA_V, d), ((0, 0), (LANES - MLA_V, 0), (0, 0)))
    wa = wa.reshape(HEADS_WIDTH, d).astype(BF16)
    wb = jnp.pad(w_gqa_branch.reshape(GQA_HEADS, GQA_HEAD_DIM, d), ((0, 0), (0, LANES - GQA_HEAD_DIM), (0, 0)))
    wb = wb.reshape(HEADS_WIDTH, d).astype(BF16)

    wr = jnp.pad(w_router, ((0, 0), (0, LANES - N_EXPERTS)))
    wr_hi = wr.astype(BF16)
    wr_lo = (wr - wr_hi.astype(F32)).astype(BF16)
    br = jnp.concatenate([b_router.astype(F32), jnp.full((LANES - N_EXPERTS,), -1e30, F32)]).reshape(1, LANES)

    tabs_a = _rope_tables(s, MLA_ROPE, MLA_NOPE, MLA_QK, MLA_ROPE)
    tabs_b = _rope_tables(s, GQA_HEAD_DIM, 0, GQA_HEAD_DIM, GQA_HEAD_DIM)

    x2d = x.reshape(t, d)
    qa, ka, va, qb, kb, vb, ga, gb = _pre_call(
        x2d, s, row(g_attn_norm), win, row(g_q_lat), wqup, row(g_kv_lat), wkvup,
        gk, ggk, row(b_gate), tabs_a + tabs_b)

    def seq(a):
        return a.reshape(b, s, a.shape[-1])

    ya = _flash_call(seq(qa), gq, tabs_a, seq(ka), seq(va), 1, MLA_ONES_LANE, MLA_QK, "attention_mla")
    yb = _flash_call(seq(qb), ggq, tabs_b, seq(kb), seq(vb), GQA_GROUP, GQA_ONES_LANE, GQA_HEAD_DIM,
                     "attention_gqa")

    x1, h2, aff, aff_rows = _post_call(x2d, ya.reshape(t, HEADS_WIDTH), yb.reshape(t, HEADS_WIDTH), ga, gb,
                                       wa, wb, w_out.astype(BF16), row(g_ffn_norm), wr_hi, wr_lo, br)

    slot_t, gate_t, slot_r, meta = _route_call(seq(aff), aff_rows, cap)
    meta = meta.reshape(b, s // MOE_TC, 1, LANES)
    xe = _dispatch_call(meta, slot_r, seq(h2), cap)
    y = _expert_call(xe, w_exp_gate, w_exp_up, w_exp_down, cap)
    return _combine_call(meta, seq(x1), slot_t, gate_t, y, cap)
```

```python
import functools

import jax
import jax.numpy as jnp
import numpy as np
from jax import lax
from jax.experimental import pallas as pl
from jax.experimental.pallas import tpu as pltpu

F32 = jnp.float32
BF16 = jnp.bfloat16
I32 = jnp.int32

D_MODEL = 1024
GRID_W = 64
ROPE_THETA = 10000.0
EPS = 1e-6

MLA_HEADS = 8
MLA_NOPE = 64
MLA_ROPE = 32
MLA_QK = MLA_NOPE + MLA_ROPE
MLA_V = 64
Q_LORA = 768
KV_LORA = 256

GQA_HEADS = 8
GQA_KV_HEADS = 2
GQA_HEAD_DIM = 64
GQA_GROUP = GQA_HEADS // GQA_KV_HEADS

N_EXPERTS = 16
CAPACITY_FACTOR = 2
EXPERT_FF = 1024

MIN_NORMAL_BITS = 0x00800000
LANES = 128
HEADS_WIDTH = MLA_HEADS * LANES

_C_QLAT = 0
_C_KVLAT = _C_QLAT + Q_LORA
_C_KROPE = _C_KVLAT + KV_LORA
_C_QG = _C_KROPE + LANES
_C_KG = _C_QG + HEADS_WIDTH
_C_VG = _C_KG + GQA_KV_HEADS * LANES
_C_GA = _C_VG + GQA_KV_HEADS * LANES
_C_GB = _C_GA + D_MODEL
_C_END = _C_GB + D_MODEL

MLA_ONES_LANE = 0
GQA_ONES_LANE = GQA_HEAD_DIM
LOG2_E = 1.4426950408889634

PRE_TM = 256
ATT_TQ = 256
ATT_UNITS = 16
ATT_TK = 512
ATT_TK2 = 2048
ROUTE_BLK = 512
MOE_TC = 256
MOE_WIN = 64
SLOT_ALIGN = 16
META_FLAG_LANE = N_EXPERTS
VMEM_LIMIT = 56 * 1024 * 1024

_NT = (((1,), (1,)), ((), ()))


def _dot(a, b):
    return jnp.dot(a, b, preferred_element_type=F32)


def _rms(x, width):
    return x * lax.rsqrt(jnp.sum(x * x, axis=-1, keepdims=True) * (1.0 / width) + EPS)


def _rope(x, c, sa, sb):
    return x * c + pltpu.roll(x, LANES - 1, 1) * sa + pltpu.roll(x, 1, 1) * sb


def _pre_kernel(x_ref, gattn_ref, win_ref, gql_ref, wqup_ref, gkvl_ref, wkvup_ref,
                gk_ref, ggk_ref, bgate_ref,
                cm_ref, sam_ref, sbm_ref, cg_ref, sag_ref, sbg_ref,
                qa_ref, ka_ref, va_ref, qb_ref, kb_ref, vb_ref, ga_ref, gb_ref):
    x = x_ref[...]
    h = (_rms(x, D_MODEL) * gattn_ref[...]).astype(BF16)

    def proj(lo, hi):
        return _dot(h, win_ref[:, lo:hi])

    lane = lax.broadcasted_iota(I32, (x.shape[0], LANES), 1)
    low_half = lane < GQA_HEAD_DIM
    cm, sam, sbm = cm_ref[...], sam_ref[...], sbm_ref[...]
    cg, sag, sbg = cg_ref[...], sag_ref[...], sbg_ref[...]

    latents = proj(_C_QLAT, _C_QG)
    c_q = (_rms(latents[:, _C_QLAT:_C_KVLAT], Q_LORA) * gql_ref[...]).astype(BF16)
    qa_ref[...] = _dot(c_q, wqup_ref[...])

    c_kv = (_rms(latents[:, _C_KVLAT:_C_KROPE], KV_LORA) * gkvl_ref[...]).astype(BF16)
    k_rope = latents[:, _C_KROPE:_C_QG]
    kv_up = _dot(c_kv, wkvup_ref[...])
    gk = gk_ref[...]
    for hd in range(MLA_HEADS):
        kv = kv_up[:, hd * LANES:(hd + 1) * LANES]
        kseg = jnp.where(low_half, kv, 0.0) + k_rope
        kseg = _rms(kseg, MLA_QK) * gk
        ka_ref[:, hd * LANES:(hd + 1) * LANES] = _rope(kseg, cm, sam, sbm).astype(BF16)
        vseg = jnp.where(lane == MLA_ONES_LANE, 1.0, jnp.where(low_half, 0.0, kv))
        va_ref[:, hd * LANES:(hd + 1) * LANES] = vseg.astype(BF16)

    qb_ref[...] = proj(_C_QG, _C_KG)
    ggk = ggk_ref[...]
    kvg = proj(_C_KG, _C_GA)
    kv_w = GQA_KV_HEADS * LANES
    for hd in range(GQA_KV_HEADS):
        seg = _rms(kvg[:, hd * LANES:(hd + 1) * LANES], GQA_HEAD_DIM) * ggk
        kb_ref[:, hd * LANES:(hd + 1) * LANES] = _rope(seg, cg, sag, sbg).astype(BF16)
        vseg = jnp.where(lane == GQA_ONES_LANE, 1.0, kvg[:, kv_w + hd * LANES:kv_w + (hd + 1) * LANES])
        vb_ref[:, hd * LANES:(hd + 1) * LANES] = vseg.astype(BF16)

    gates = jax.nn.sigmoid(proj(_C_GA, _C_END) + bgate_ref[...])
    ga_ref[...] = gates[:, :D_MODEL].astype(BF16)
    gb_ref[...] = gates[:, D_MODEL:].astype(BF16)


def _pre_call(x2d, seq, gattn, win, gql, wqup, gkvl, wkvup, gk, ggk, bgate, tabs):
    t = x2d.shape[0]
    tm = PRE_TM
    n_seq_tiles = seq // tm

    def full(a):
        return pl.BlockSpec(a.shape, lambda i: (0,) * a.ndim)

    def tok(width):
        return pl.BlockSpec((tm, width), lambda i: (i, 0))

    tab_spec = pl.BlockSpec((tm, LANES), lambda i: (i % n_seq_tiles, 0))
    params = (gattn, win, gql, wqup, gkvl, wkvup, gk, ggk, bgate)
    kv_w = GQA_KV_HEADS * LANES
    outs = ((HEADS_WIDTH, F32), (HEADS_WIDTH, BF16), (HEADS_WIDTH, BF16), (HEADS_WIDTH, F32),
            (kv_w, BF16), (kv_w, BF16), (D_MODEL, BF16), (D_MODEL, BF16))
    return pl.pallas_call(
        _pre_kernel,
        grid=(t // tm,),
        in_specs=[tok(D_MODEL)] + [full(a) for a in params] + [tab_spec] * 6,
        out_specs=[tok(w) for w, _ in outs],
        out_shape=[jax.ShapeDtypeStruct((t, w), dt) for w, dt in outs],
        compiler_params=pltpu.CompilerParams(
            dimension_semantics=("arbitrary",), vmem_limit_bytes=VMEM_LIMIT),
        name="pre_attention",
    )(x2d, *params, *tabs)


def _flash_kernel(q_ref, gain_ref, c_ref, sa_ref, sb_ref, k_ref, v_ref, o_ref, s0_ref, s1_ref, *,
                  units, ones_lane, head_dim):
    tq = s0_ref.shape[0]
    s_len = k_ref.shape[1]
    s_bufs = (s0_ref, s1_ref)
    q_scale = head_dim ** -0.5 * LOG2_E

    def q_of(u):
        r, g = units[u]
        rows = slice(r * tq, (r + 1) * tq)
        q = _rms(q_ref[0, rows, g * LANES:(g + 1) * LANES], head_dim) * gain_ref[...]
        q = _rope(q, c_ref[rows, :], sa_ref[rows, :], sb_ref[rows, :]) * q_scale
        return q.astype(BF16)

    def score_pass(u, j, q, m128):
        cols = slice(j * ATT_TK, (j + 1) * ATT_TK)
        s = lax.dot_general(q, k_ref[0, cols, :], _NT, preferred_element_type=F32)
        s_bufs[u % 2][:, cols] = s
        for c in range(ATT_TK // LANES):
            m128 = jnp.maximum(m128, s[:, c * LANES:(c + 1) * LANES])
        return m128

    def value_pass(u, j, m, acc):
        cols = slice(j * ATT_TK2, (j + 1) * ATT_TK2)
        p = jnp.exp2(s_bufs[u % 2][:, cols] - m)
        return acc + _dot(p.astype(BF16), v_ref[0, cols, :])

    neg = jnp.full((tq, LANES), -1e30, F32)
    m128 = neg
    q_first = q_of(0)
    for j in range(s_len // ATT_TK):
        m128 = score_pass(0, j, q_first, m128)
    ratio = ATT_TK2 // ATT_TK
    for u in range(len(units)):
        m = jnp.max(m128, axis=-1, keepdims=True)
        acc = jnp.zeros((tq, LANES), F32)
        has_next = u + 1 < len(units)
        if has_next:
            q_next = q_of(u + 1)
            m128 = neg
        for j2 in range(s_len // ATT_TK2):
            acc = value_pass(u, j2, m, acc)
            if has_next:
                for j in range(j2 * ratio, (j2 + 1) * ratio):
                    m128 = score_pass(u + 1, j, q_next, m128)
        row_sum = acc[:, ones_lane:ones_lane + 1]
        r, g = units[u]
        o_ref[0, r * tq:(r + 1) * tq, g * LANES:(g + 1) * LANES] = (acc / row_sum).astype(BF16)


def _flash_call(q, gain, tabs, k, v, group, ones_lane, head_dim, name):
    b, s, qw = q.shape
    n_kv = k.shape[2] // LANES
    assert qw == n_kv * group * LANES and ATT_UNITS % group == 0
    assert s % ATT_TK2 == 0 and ATT_TK2 % ATT_TK == 0
    row_tiles = ATT_UNITS // group
    rows = ATT_TQ * row_tiles
    assert s % rows == 0
    units = tuple((r, g) for r in range(row_tiles) for g in range(group))
    q_spec = pl.BlockSpec((1, rows, group * LANES), lambda bi, kh, qi: (bi, qi, kh))
    kv_spec = pl.BlockSpec((1, s, LANES), lambda bi, kh, qi: (bi, 0, kh))
    tab_spec = pl.BlockSpec((rows, LANES), lambda bi, kh, qi: (qi, 0))
    gain_spec = pl.BlockSpec((1, LANES), lambda bi, kh, qi: (0, 0))
    return pl.pallas_call(
        functools.partial(_flash_kernel, units=units, ones_lane=ones_lane, head_dim=head_dim),
        grid=(b, n_kv, s // rows),
        in_specs=[q_spec, gain_spec, tab_spec, tab_spec, tab_spec, kv_spec, kv_spec],
        out_specs=q_spec,
        out_shape=jax.ShapeDtypeStruct(q.shape, BF16),
        scratch_shapes=[pltpu.VMEM((ATT_TQ, s), F32), pltpu.VMEM((ATT_TQ, s), F32)],
        compiler_params=pltpu.CompilerParams(
            dimension_semantics=("arbitrary", "arbitrary", "arbitrary"),
            vmem_limit_bytes=VMEM_LIMIT),
        name=name,
    )(q, gain, *tabs, k, v)


def _post_kernel(x_ref, ya_ref, yb_ref, ga_ref, gb_ref, wa_ref, wb_ref, wout_ref,
                 gffn_ref, wrh_ref, wrl_ref, br_ref, x1_ref, h2_ref, aff_ref, aff_rows_ref):
    a = _dot(ya_ref[...], wa_ref[...])
    b = _dot(yb_ref[...], wb_ref[...])
    merged = ga_ref[...].astype(F32) * a + gb_ref[...].astype(F32) * b
    x1 = x_ref[...] + _dot(merged.astype(BF16), wout_ref[...])
    x1_ref[...] = x1
    h2 = _rms(x1, D_MODEL) * gffn_ref[...]
    h2_hi = h2.astype(BF16)
    h2_ref[...] = h2_hi
    h2_lo = (h2 - h2_hi.astype(F32)).astype(BF16)
    wrh = wrh_ref[...]
    logits = _dot(h2_hi, wrh) + _dot(h2_lo, wrh) + _dot(h2_hi, wrl_ref[...]) + br_ref[...]
    e = jnp.exp(logits - jnp.max(logits, axis=-1, keepdims=True))
    aff = e / jnp.sum(e, axis=-1, keepdims=True)
    aff_ref[...] = aff
    aff_rows_ref[...] = jnp.transpose(aff)[:N_EXPERTS]


def _post_call(x2d, ya, yb, ga, gb, wa, wb, wout, gffn, wrh, wrl, br):
    t = x2d.shape[0]
    tm = PRE_TM

    def full(a):
        return pl.BlockSpec(a.shape, lambda i: (0,) * a.ndim)

    def tok(width):
        return pl.BlockSpec((tm, width), lambda i: (i, 0))

    params = (wa, wb, wout, gffn, wrh, wrl, br)
    return pl.pallas_call(
        _post_kernel,
        grid=(t // tm,),
        in_specs=[tok(D_MODEL)] * 5 + [full(a) for a in params],
        out_specs=[tok(D_MODEL), tok(D_MODEL), tok(LANES),
                   pl.BlockSpec((N_EXPERTS, tm), lambda i: (0, i))],
        out_shape=[jax.ShapeDtypeStruct((t, D_MODEL), F32),
                   jax.ShapeDtypeStruct((t, D_MODEL), BF16),
                   jax.ShapeDtypeStruct((t, LANES), F32),
                   jax.ShapeDtypeStruct((N_EXPERTS, t), F32)],
        compiler_params=pltpu.CompilerParams(
            dimension_semantics=("arbitrary",), vmem_limit_bytes=VMEM_LIMIT),
        name="post_attention",
    )(x2d, ya, yb, ga, gb, *params)


def _window_meta(start, end, cap):
    lane = lax.broadcasted_iota(I32, start.shape, 1)
    first = jnp.minimum(jnp.floor(start * (1.0 / SLOT_ALIGN)) * SLOT_ALIGN, float(cap - MOE_WIN))
    over = jnp.where((lane < N_EXPERTS) & (end - first > MOE_WIN), 1.0, 0.0)
    flag = jnp.max(over, axis=-1, keepdims=True)
    return jnp.where(lane == META_FLAG_LANE, flag, first).astype(I32)


def _route_kernel(aff_ref, aff_rows_ref, slot_t_ref, gate_t_ref, slot_r_ref, meta_ref, *, cap):
    s = aff_ref.shape[1]

    def count(mask):
        return jnp.sum(jnp.where(mask, 1.0, 0.0), axis=1, keepdims=True)

    def as_float(bits):
        return lax.bitcast_convert_type(bits, F32)

    def search(i, bits):
        cand = bits | jnp.left_shift(jnp.int32(1), 29 - i)
        return jnp.where(count(aff_rows_ref[...] >= as_float(cand)) >= cap, cand, bits)

    bits = lax.fori_loop(0, 30, search, jnp.zeros((N_EXPERTS, 1), I32))
    is_normal = bits >= MIN_NORMAL_BITS
    thr_col = jnp.where(is_normal, as_float(bits), 0.0)
    next_col = as_float(jnp.where(is_normal, bits + 1, MIN_NORMAL_BITS))
    need_col = cap - count(aff_rows_ref[...] >= next_col)

    def to_lanes(col):
        on_diag = (lax.broadcasted_iota(I32, (N_EXPERTS, LANES), 0)
                   == lax.broadcasted_iota(I32, (N_EXPERTS, LANES), 1))
        return jnp.sum(jnp.where(on_diag, col, 0.0), axis=0, keepdims=True)

    thr, thr_next, need = to_lanes(thr_col), to_lanes(next_col), to_lanes(need_col)

    blk = ROUTE_BLK
    tri = (lax.broadcasted_iota(I32, (blk, blk), 0) >= lax.broadcasted_iota(I32, (blk, blk), 1)).astype(BF16)
    eye = (lax.broadcasted_iota(I32, (LANES, LANES), 0) == lax.broadcasted_iota(I32, (LANES, LANES), 1)).astype(BF16)
    carry_eq = jnp.zeros((1, LANES), F32)
    carry_sel = jnp.zeros((1, LANES), F32)
    for c in range(s // blk):
        rows = slice(c * blk, (c + 1) * blk)
        ab = aff_ref[0, rows, :]
        gt = ab >= thr_next
        eq = jnp.where((ab >= thr) & jnp.logical_not(gt), 1.0, 0.0)
        eq_incl = _dot(tri, eq.astype(BF16)) + carry_eq
        carry_eq = eq_incl[blk - 1:blk, :]
        sel = jnp.where(gt | ((eq > 0.0) & (eq_incl - eq < need)), 1.0, 0.0)
        sel_incl = _dot(tri, sel.astype(BF16)) + carry_sel
        edge = carry_sel
        for k in range(blk // MOE_TC):
            nxt = sel_incl[(k + 1) * MOE_TC - 1:(k + 1) * MOE_TC, :]
            chunk = c * (blk // MOE_TC) + k
            meta_ref[0, chunk:chunk + 1, :] = _window_meta(edge, nxt, cap)
            edge = nxt
        carry_sel = sel_incl[blk - 1:blk, :]
        slot1 = sel * sel_incl
        slot_t_ref[0, rows, :] = slot1.astype(I32) - 1
        gate_t_ref[0, rows, :] = sel * aff_ref[0, rows, :]
        hi = jnp.floor(slot1 * (1.0 / 32.0))
        lo = slot1 - 32.0 * hi
        rows_hi = lax.dot_general(eye, hi.astype(BF16), _NT, preferred_element_type=F32)
        rows_lo = lax.dot_general(eye, lo.astype(BF16), _NT, preferred_element_type=F32)
        slot_rows = (32.0 * rows_hi + rows_lo).astype(I32) - 1
        slot_r_ref[0, :, rows] = slot_rows[:N_EXPERTS]


def _route_call(aff, aff_rows, cap):
    b, s, _ = aff.shape
    tok_spec = pl.BlockSpec((1, s, LANES), lambda bi: (bi, 0, 0))
    return pl.pallas_call(
        functools.partial(_route_kernel, cap=cap),
        grid=(b,),
        in_specs=[tok_spec, pl.BlockSpec((N_EXPERTS, s), lambda bi: (0, bi))],
        out_specs=[tok_spec, tok_spec, pl.BlockSpec((1, N_EXPERTS, s), lambda bi: (bi, 0, 0)),
                   pl.BlockSpec((1, s // MOE_TC, LANES), lambda bi: (bi, 0, 0))],
        out_shape=[jax.ShapeDtypeStruct((b, s, LANES), I32),
                   jax.ShapeDtypeStruct((b, s, LANES), F32),
                   jax.ShapeDtypeStruct((b, N_EXPERTS, s), I32),
                   jax.ShapeDtypeStruct((b, s // MOE_TC, LANES), I32)],
        compiler_params=pltpu.CompilerParams(
            dimension_semantics=("arbitrary",), vmem_limit_bytes=VMEM_LIMIT),
        name="route",
    )(aff, aff_rows)


def _meta_spec():
    return pl.BlockSpec((1, 1, 1, LANES), lambda bi, ci: (bi, ci, 0, 0), memory_space=pltpu.SMEM)


def _dispatch_kernel(meta_ref, slot_r_ref, h2_ref, xe_ref, *, cap):
    tc = h2_ref.shape[1]

    @pl.when(pl.program_id(1) == 0)
    def _():
        xe_ref[...] = jnp.zeros_like(xe_ref)

    slot_rows = slot_r_ref[0]
    h2c = h2_ref[0]

    def add_rows(first_row, rows):
        n = rows.shape[0]
        cur = xe_ref[0, pl.ds(first_row, n), :].astype(F32)
        xe_ref[0, pl.ds(first_row, n), :] = (cur + rows).astype(BF16)

    fits = meta_ref[0, 0, 0,META_FLAG_LANE] == 0

    @pl.when(fits)
    def _():
        firsts = [pl.multiple_of(meta_ref[0, 0, 0,e], SLOT_ALIGN) for e in range(N_EXPERTS)]
        win_ids = lax.broadcasted_iota(I32, (MOE_WIN, tc), 0)
        half = N_EXPERTS // 2
        for lo in (0, half):
            pick = jnp.concatenate(
                [jnp.where(win_ids == slot_rows[e:e + 1, :] - firsts[e], 1.0, 0.0).astype(BF16)
                 for e in range(lo, lo + half)], axis=0)
            got = _dot(pick, h2c)
            for i, e in enumerate(range(lo, lo + half)):
                add_rows(e * cap + firsts[e], got[i * MOE_WIN:(i + 1) * MOE_WIN])

    @pl.when(jnp.logical_not(fits))
    def _():
        slot_ids = lax.broadcasted_iota(I32, (cap, tc), 0)
        for e in range(N_EXPERTS):
            pick = jnp.where(slot_ids == slot_rows[e:e + 1, :], 1.0, 0.0).astype(BF16)
            add_rows(e * cap, _dot(pick, h2c))


def _dispatch_call(meta, slot_r, h2, cap):
    b, s, d = h2.shape
    tc = MOE_TC
    return pl.pallas_call(
        functools.partial(_dispatch_kernel, cap=cap),
        grid=(b, s // tc),
        in_specs=[_meta_spec(),
                  pl.BlockSpec((1, N_EXPERTS, tc), lambda bi, ci: (bi, 0, ci)),
                  pl.BlockSpec((1, tc, d), lambda bi, ci: (bi, ci, 0))],
        out_specs=pl.BlockSpec((1, N_EXPERTS * cap, d), lambda bi, ci: (bi, 0, 0)),
        out_shape=jax.ShapeDtypeStruct((b, N_EXPERTS * cap, d), BF16),
        compiler_params=pltpu.CompilerParams(
            dimension_semantics=("arbitrary", "arbitrary"), vmem_limit_bytes=VMEM_LIMIT),
        name="dispatch",
    )(meta, slot_r, h2)


def _expert_kernel(xe_ref, wg_ref, wu_ref, wd_ref, y_ref, wg_bf, wu_bf, wd_bf):
    @pl.when(pl.program_id(1) == 0)
    def _():
        wg_bf[...] = wg_ref[0].astype(BF16)
        wu_bf[...] = wu_ref[0].astype(BF16)
        wd_bf[...] = wd_ref[0].astype(BF16)

    xe = xe_ref[0]
    a = _dot(xe, wg_bf[...])
    u = _dot(xe, wu_bf[...])
    act = (a * jax.nn.sigmoid(a) * u).astype(BF16)
    y_ref[0] = _dot(act, wd_bf[...]).astype(BF16)


def _expert_call(xe, wg, wu, wd, cap):
    b, _, d = xe.shape
    up_spec = pl.BlockSpec((1, d, EXPERT_FF), lambda e, bi: (e, 0, 0))
    down_spec = pl.BlockSpec((1, EXPERT_FF, d), lambda e, bi: (e, 0, 0))
    tok_spec = pl.BlockSpec((1, cap, d), lambda e, bi: (bi, e, 0))
    return pl.pallas_call(
        _expert_kernel,
        grid=(N_EXPERTS, b),
        in_specs=[tok_spec, up_spec, up_spec, down_spec],
        out_specs=tok_spec,
        out_shape=jax.ShapeDtypeStruct(xe.shape, BF16),
        scratch_shapes=[pltpu.VMEM((d, EXPERT_FF), BF16), pltpu.VMEM((d, EXPERT_FF), BF16),
                        pltpu.VMEM((EXPERT_FF, d), BF16)],
        compiler_params=pltpu.CompilerParams(
            dimension_semantics=("arbitrary", "arbitrary"), vmem_limit_bytes=VMEM_LIMIT),
        name="experts",
    )(xe, wg, wu, wd)


def _combine_kernel(meta_ref, x1_ref, slot_t_ref, gate_t_ref, y_ref, o_ref, ywin_ref, *, cap):
    tc = x1_ref.shape[1]
    slot_t = slot_t_ref[0]
    gate_t = gate_t_ref[0]
    fits = meta_ref[0, 0, 0,META_FLAG_LANE] == 0

    @pl.when(fits)
    def _():
        per_block = LANES // MOE_WIN
        lane = lax.broadcasted_iota(I32, (tc, LANES), 1)
        pieces = []
        for blk in range(N_EXPERTS // per_block):
            want = jnp.full((tc, LANES), -1, I32)
            gate = jnp.zeros((tc, LANES), F32)
            for i in range(per_block):
                e = blk * per_block + i
                first = pl.multiple_of(meta_ref[0, 0, 0, e], SLOT_ALIGN)
                ywin_ref[e * MOE_WIN:(e + 1) * MOE_WIN, :] = y_ref[0, pl.ds(e * cap + first, MOE_WIN), :]
                mine = (lane >= i * MOE_WIN) & (lane < (i + 1) * MOE_WIN)
                want = jnp.where(mine, slot_t[:, e:e + 1] - first + i * MOE_WIN, want)
                gate = jnp.where(mine, gate_t[:, e:e + 1], gate)
            pieces.append(jnp.where(lane == want, gate, 0.0).astype(BF16))
        o_ref[0] = x1_ref[0] + _dot(jnp.concatenate(pieces, axis=1), ywin_ref[...])

    @pl.when(jnp.logical_not(fits))
    def _():
        acc = x1_ref[0]
        slot_ids = lax.broadcasted_iota(I32, (tc, cap), 1)
        for e in range(N_EXPERTS):
            place = jnp.where(slot_ids == slot_t[:, e:e + 1], gate_t[:, e:e + 1], 0.0).astype(BF16)
            acc = acc + _dot(place, y_ref[0, e * cap:(e + 1) * cap, :])
        o_ref[0] = acc


def _combine_call(meta, x1, slot_t, gate_t, y, cap):
    b, s, d = x1.shape
    tc = MOE_TC
    return pl.pallas_call(
        functools.partial(_combine_kernel, cap=cap),
        grid=(b, s // tc),
        in_specs=[_meta_spec(),
                  pl.BlockSpec((1, tc, d), lambda bi, ci: (bi, ci, 0)),
                  pl.BlockSpec((1, tc, LANES), lambda bi, ci: (bi, ci, 0)),
                  pl.BlockSpec((1, tc, LANES), lambda bi, ci: (bi, ci, 0)),
                  pl.BlockSpec((1, N_EXPERTS * cap, d), lambda bi, ci: (bi, 0, 0))],
        out_specs=pl.BlockSpec((1, tc, d), lambda bi, ci: (bi, ci, 0)),
        out_shape=jax.ShapeDtypeStruct((b, s, d), F32),
        scratch_shapes=[pltpu.VMEM((N_EXPERTS * MOE_WIN, d), BF16)],
        compiler_params=pltpu.CompilerParams(
            dimension_semantics=("arbitrary", "arbitrary"), vmem_limit_bytes=VMEM_LIMIT),
        name="combine",
    )(meta, x1, slot_t, gate_t, y)


def _rope_tables(n, rot_dim, lane_lo, lane_hi, period):
    rows = n // GRID_W
    row = jnp.broadcast_to(jnp.arange(rows)[:, None], (rows, GRID_W)).reshape(n).astype(F32)
    col = jnp.broadcast_to(jnp.arange(GRID_W)[None, :], (rows, GRID_W)).reshape(n).astype(F32)
    nf = rot_dim // 4
    inv = ROPE_THETA ** (-jnp.arange(nf, dtype=F32) / nf)
    ang = jnp.concatenate([row[:, None] * inv, col[:, None] * inv], axis=-1)
    cos, sin = jnp.cos(ang), jnp.sin(ang)
    lane = np.arange(LANES)
    active = (lane >= lane_lo) & (lane < lane_hi)
    pair = np.where(active, ((lane - lane_lo) % period) // 2, 0)
    even = (lane % 2) == 0
    c = jnp.where(active[None, :], cos[:, pair], 1.0)
    sa = jnp.where((active & even)[None, :], -sin[:, pair], 0.0)
    sb = jnp.where((active & ~even)[None, :], sin[:, pair], 0.0)
    return c, sa, sb


def _pad_cols(w, heads, width):
    k = w.shape[0]
    w = w.reshape(k, heads, width)
    return jnp.pad(w, ((0, 0), (0, 0), (0, LANES - width))).reshape(k, heads * LANES)


def kernel(x, g_attn_norm, w_in, b_gate, g_q_lat, w_q_up, g_kv_lat, w_kv_up, g_mla_qnorm, g_mla_knorm, g_gqa_qnorm, g_gqa_knorm, w_mla_branch, w_gqa_branch, w_out, g_ffn_norm, w_router, b_router, w_exp_gate, w_exp_up, w_exp_down):
    b, s, d = x.shape
    assert d == D_MODEL and s % max(PRE_TM, ROUTE_BLK) == 0 and ROUTE_BLK % MOE_TC == 0
    cap = CAPACITY_FACTOR * s // N_EXPERTS
    assert cap >= MOE_WIN and cap % SLOT_ALIGN == 0
    t = b * s

    splits = np.cumsum([Q_LORA, KV_LORA, MLA_ROPE, GQA_HEADS * GQA_HEAD_DIM,
                        GQA_KV_HEADS * GQA_HEAD_DIM, GQA_KV_HEADS * GQA_HEAD_DIM, D_MODEL])
    w_ql, w_kvl, w_kr, w_qg, w_kg, w_vg, w_ga, w_gb = jnp.split(w_in, splits, axis=1)
    w_kr = jnp.pad(w_kr, ((0, 0), (MLA_NOPE, LANES - MLA_QK)))
    w_qg = _pad_cols(w_qg, GQA_HEADS, GQA_HEAD_DIM)
    w_kg = _pad_cols(w_kg, GQA_KV_HEADS, GQA_HEAD_DIM)
    w_vg = _pad_cols(w_vg, GQA_KV_HEADS, GQA_HEAD_DIM)
    win = jnp.concatenate([w_ql, w_kvl, w_kr, w_qg, w_kg, w_vg, w_ga, w_gb], axis=1).astype(BF16)
    wqup = _pad_cols(w_q_up, MLA_HEADS, MLA_QK).astype(BF16)
    wkvup = w_kv_up.astype(BF16)

    def row(v):
        return v.reshape(1, -1).astype(F32)

    gq = row(jnp.pad(g_mla_qnorm, (0, LANES - MLA_QK)))
    gk = row(jnp.pad(g_mla_knorm, (0, LANES - MLA_QK)))
    ggq = row(jnp.pad(g_gqa_qnorm, (0, LANES - GQA_HEAD_DIM)))
    ggk = row(jnp.pad(g_gqa_knorm, (0, LANES - GQA_HEAD_DIM)))

    wa = jnp.pad(w_mla_branch.reshape(MLA_HEADS, MLA_V, d), ((0, 0), (LANES - MLA_V, 0), (0, 0)))
    wa = wa.reshape(HEADS_WIDTH, d).astype(BF16)
    wb = jnp.pad(w_gqa_branch.reshape(GQA_HEADS, GQA_HEAD_DIM, d), ((0, 0), (0, LANES - GQA_HEAD_DIM), (0, 0)))
    wb = wb.reshape(HEADS_WIDTH, d).astype(BF16)

    wr = jnp.pad(w_router, ((0, 0), (0, LANES - N_EXPERTS)))
    wr_hi = wr.astype(BF16)
    wr_lo = (wr - wr_hi.astype(F32)).astype(BF16)
    br = jnp.concatenate([b_router.astype(F32), jnp.full((LANES - N_EXPERTS,), -1e30, F32)]).reshape(1, LANES)

    tabs_a = _rope_tables(s, MLA_ROPE, MLA_NOPE, MLA_QK, MLA_ROPE)
    tabs_b = _rope_tables(s, GQA_HEAD_DIM, 0, GQA_HEAD_DIM, GQA_HEAD_DIM)

    x2d = x.reshape(t, d)
    qa, ka, va, qb, kb, vb, ga, gb = _pre_call(
        x2d, s, row(g_attn_norm), win, row(g_q_lat), wqup, row(g_kv_lat), wkvup,
        gk, ggk, row(b_gate), tabs_a + tabs_b)

    def seq(a):
        return a.reshape(b, s, a.shape[-1])

    ya = _flash_call(seq(qa), gq, tabs_a, seq(ka), seq(va), 1, MLA_ONES_LANE, MLA_QK, "attention_mla")
    yb = _flash_call(seq(qb), ggq, tabs_b, seq(kb), seq(vb), GQA_GROUP, GQA_ONES_LANE, GQA_HEAD_DIM,
                     "attention_gqa")

    x1, h2, aff, aff_rows = _post_call(x2d, ya.reshape(t, HEADS_WIDTH), yb.reshape(t, HEADS_WIDTH), ga, gb,
                                       wa, wb, w_out.astype(BF16), row(g_ffn_norm), wr_hi, wr_lo, br)

    slot_t, gate_t, slot_r, meta = _route_call(seq(aff), aff_rows, cap)
    meta = meta.reshape(b, s // MOE_TC, 1, LANES)
    xe = _dispatch_call(meta, slot_r, seq(h2), cap)
    y = _expert_call(xe, w_exp_gate, w_exp_up, w_exp_down, cap)
    return _combine_call(meta, seq(x1), slot_t, gate_t, y, cap)
```

```python
import functools

import jax
import jax.numpy as jnp
import numpy as np
from jax import lax
from jax.experimental import pallas as pl
from jax.experimental.pallas import tpu as pltpu

F32 = jnp.float32
BF16 = jnp.bfloat16
I32 = jnp.int32

D_MODEL = 1024
GRID_W = 64
ROPE_THETA = 10000.0
EPS = 1e-6

MLA_HEADS = 8
MLA_NOPE = 64
MLA_ROPE = 32
MLA_QK = MLA_NOPE + MLA_ROPE
MLA_V = 64
Q_LORA = 768
KV_LORA = 256

GQA_HEADS = 8
GQA_KV_HEADS = 2
GQA_HEAD_DIM = 64
GQA_GROUP = GQA_HEADS // GQA_KV_HEADS

N_EXPERTS = 16
CAPACITY_FACTOR = 2
EXPERT_FF = 1024

MIN_NORMAL_BITS = 0x00800000
LANES = 128
HEADS_WIDTH = MLA_HEADS * LANES

_C_QLAT = 0
_C_KVLAT = _C_QLAT + Q_LORA
_C_KROPE = _C_KVLAT + KV_LORA
_C_QG = _C_KROPE + LANES
_C_KG = _C_QG + HEADS_WIDTH
_C_VG = _C_KG + GQA_KV_HEADS * LANES
_C_GA = _C_VG + GQA_KV_HEADS * LANES
_C_GB = _C_GA + D_MODEL
_C_END = _C_GB + D_MODEL

MLA_ONES_LANE = 0
GQA_ONES_LANE = GQA_HEAD_DIM
LOG2_E = 1.4426950408889634

PRE_TM = 256
POST_TM = 512
EXPERT_SEQS = 2
ATT_TQ = 256
ATT_UNITS = 16
ATT_TK = 512
ATT_TK2 = 2048
ROUTE_BLK = 512
MOE_TC = 256
MOE_WIN = 64
SLOT_ALIGN = 16
META_FLAG_LANE = N_EXPERTS
VMEM_LIMIT = 56 * 1024 * 1024

_NT = (((1,), (1,)), ((), ()))


def _dot(a, b):
    return jnp.dot(a, b, preferred_element_type=F32)


def _rms(x, width):
    return x * lax.rsqrt(jnp.sum(x * x, axis=-1, keepdims=True) * (1.0 / width) + EPS)


def _rope(x, c, sa, sb):
    return x * c + pltpu.roll(x, LANES - 1, 1) * sa + pltpu.roll(x, 1, 1) * sb


def _pre_kernel(x_ref, gattn_ref, win_ref, gql_ref, wqup_ref, gkvl_ref, wkvup_ref,
                gk_ref, ggk_ref, bgate_ref,
                cm_ref, sam_ref, sbm_ref, cg_ref, sag_ref, sbg_ref,
                qa_ref, ka_ref, va_ref, qb_ref, kb_ref, vb_ref, ga_ref, gb_ref):
    x = x_ref[...]
    h = (_rms(x, D_MODEL) * gattn_ref[...]).astype(BF16)

    def proj(lo, hi):
        return _dot(h, win_ref[:, lo:hi])

    lane = lax.broadcasted_iota(I32, (x.shape[0], LANES), 1)
    low_half = lane < GQA_HEAD_DIM
    cm, sam, sbm = cm_ref[...], sam_ref[...], sbm_ref[...]
    cg, sag, sbg = cg_ref[...], sag_ref[...], sbg_ref[...]

    latents = proj(_C_QLAT, _C_QG)
    c_q = (_rms(latents[:, _C_QLAT:_C_KVLAT], Q_LORA) * gql_ref[...]).astype(BF16)
    qa_ref[...] = _dot(c_q, wqup_ref[...])

    c_kv = (_rms(latents[:, _C_KVLAT:_C_KROPE], KV_LORA) * gkvl_ref[...]).astype(BF16)
    k_rope = latents[:, _C_KROPE:_C_QG]
    kv_up = _dot(c_kv, wkvup_ref[...])
    gk = gk_ref[...]
    for hd in range(MLA_HEADS):
        kv = kv_up[:, hd * LANES:(hd + 1) * LANES]
        kseg = jnp.where(low_half, kv, 0.0) + k_rope
        kseg = _rms(kseg, MLA_QK) * gk
        ka_ref[:, hd * LANES:(hd + 1) * LANES] = _rope(kseg, cm, sam, sbm).astype(BF16)
        vseg = jnp.where(lane == MLA_ONES_LANE, 1.0, jnp.where(low_half, 0.0, kv))
        va_ref[:, hd * LANES:(hd + 1) * LANES] = vseg.astype(BF16)

    qb_ref[...] = proj(_C_QG, _C_KG)
    ggk = ggk_ref[...]
    kvg = proj(_C_KG, _C_GA)
    kv_w = GQA_KV_HEADS * LANES
    for hd in range(GQA_KV_HEADS):
        seg = _rms(kvg[:, hd * LANES:(hd + 1) * LANES], GQA_HEAD_DIM) * ggk
        kb_ref[:, hd * LANES:(hd + 1) * LANES] = _rope(seg, cg, sag, sbg).astype(BF16)
        vseg = jnp.where(lane == GQA_ONES_LANE, 1.0, kvg[:, kv_w + hd * LANES:kv_w + (hd + 1) * LANES])
        vb_ref[:, hd * LANES:(hd + 1) * LANES] = vseg.astype(BF16)

    gates = jax.nn.sigmoid(proj(_C_GA, _C_END) + bgate_ref[...])
    ga_ref[...] = gates[:, :D_MODEL].astype(BF16)
    gb_ref[...] = gates[:, D_MODEL:].astype(BF16)


def _pre_call(x2d, seq, gattn, win, gql, wqup, gkvl, wkvup, gk, ggk, bgate, tabs):
    t = x2d.shape[0]
    tm = PRE_TM
    n_seq_tiles = seq // tm

    def full(a):
        return pl.BlockSpec(a.shape, lambda i: (0,) * a.ndim)

    def tok(width):
        return pl.BlockSpec((tm, width), lambda i: (i, 0))

    tab_spec = pl.BlockSpec((tm, LANES), lambda i: (i % n_seq_tiles, 0))
    params = (gattn, win, gql, wqup, gkvl, wkvup, gk, ggk, bgate)
    kv_w = GQA_KV_HEADS * LANES
    outs = ((HEADS_WIDTH, F32), (HEADS_WIDTH, BF16), (HEADS_WIDTH, BF16), (HEADS_WIDTH, F32),
            (kv_w, BF16), (kv_w, BF16), (D_MODEL, BF16), (D_MODEL, BF16))
    return pl.pallas_call(
        _pre_kernel,
        grid=(t // tm,),
        in_specs=[tok(D_MODEL)] + [full(a) for a in params] + [tab_spec] * 6,
        out_specs=[tok(w) for w, _ in outs],
        out_shape=[jax.ShapeDtypeStruct((t, w), dt) for w, dt in outs],
        compiler_params=pltpu.CompilerParams(
            dimension_semantics=("arbitrary",), vmem_limit_bytes=VMEM_LIMIT),
        name="pre_attention",
    )(x2d, *params, *tabs)


def _flash_kernel(q_ref, gain_ref, c_ref, sa_ref, sb_ref, k_ref, v_ref, o_ref, s0_ref, s1_ref, *,
                  units, ones_lane, head_dim):
    tq = s0_ref.shape[0]
    s_len = k_ref.shape[1]
    s_bufs = (s0_ref, s1_ref)
    q_scale = head_dim ** -0.5 * LOG2_E

    def q_of(u):
        r, g = units[u]
        rows = slice(r * tq, (r + 1) * tq)
        q = _rms(q_ref[0, rows, g * LANES:(g + 1) * LANES], head_dim) * gain_ref[...]
        q = _rope(q, c_ref[rows, :], sa_ref[rows, :], sb_ref[rows, :]) * q_scale
        return q.astype(BF16)

    def score_pass(u, j, q, m128):
        cols = slice(j * ATT_TK, (j + 1) * ATT_TK)
        s = lax.dot_general(q, k_ref[0, cols, :], _NT, preferred_element_type=F32)
        s_bufs[u % 2][:, cols] = s
        for c in range(ATT_TK // LANES):
            m128 = jnp.maximum(m128, s[:, c * LANES:(c + 1) * LANES])
        return m128

    def value_pass(u, j, m, acc):
        cols = slice(j * ATT_TK2, (j + 1) * ATT_TK2)
        p = jnp.exp2(s_bufs[u % 2][:, cols] - m)
        return acc + _dot(p.astype(BF16), v_ref[0, cols, :])

    neg = jnp.full((tq, LANES), -1e30, F32)
    m128 = neg
    q_first = q_of(0)
    for j in range(s_len // ATT_TK):
        m128 = score_pass(0, j, q_first, m128)
    ratio = ATT_TK2 // ATT_TK
    for u in range(len(units)):
        m = jnp.max(m128, axis=-1, keepdims=True)
        acc = jnp.zeros((tq, LANES), F32)
        has_next = u + 1 < len(units)
        if has_next:
            q_next = q_of(u + 1)
            m128 = neg
        for j2 in range(s_len // ATT_TK2):
            acc = value_pass(u, j2, m, acc)
            if has_next:
                for j in range(j2 * ratio, (j2 + 1) * ratio):
                    m128 = score_pass(u + 1, j, q_next, m128)
        row_sum = acc[:, ones_lane:ones_lane + 1]
        r, g = units[u]
        o_ref[0, r * tq:(r + 1) * tq, g * LANES:(g + 1) * LANES] = (acc / row_sum).astype(BF16)


def _flash_call(q, gain, tabs, k, v, group, ones_lane, head_dim, name):
    b, s, qw = q.shape
    n_kv = k.shape[2] // LANES
    assert qw == n_kv * group * LANES and ATT_UNITS % group == 0
    assert s % ATT_TK2 == 0 and ATT_TK2 % ATT_TK == 0
    row_tiles = ATT_UNITS // group
    rows = ATT_TQ * row_tiles
    assert s % rows == 0
    units = tuple((r, g) for r in range(row_tiles) for g in range(group))
    q_spec = pl.BlockSpec((1, rows, group * LANES), lambda bi, kh, qi: (bi, qi, kh))
    kv_spec = pl.BlockSpec((1, s, LANES), lambda bi, kh, qi: (bi, 0, kh))
    tab_spec = pl.BlockSpec((rows, LANES), lambda bi, kh, qi: (qi, 0))
    gain_spec = pl.BlockSpec((1, LANES), lambda bi, kh, qi: (0, 0))
    return pl.pallas_call(
        functools.partial(_flash_kernel, units=units, ones_lane=ones_lane, head_dim=head_dim),
        grid=(b, n_kv, s // rows),
        in_specs=[q_spec, gain_spec, tab_spec, tab_spec, tab_spec, kv_spec, kv_spec],
        out_specs=q_spec,
        out_shape=jax.ShapeDtypeStruct(q.shape, BF16),
        scratch_shapes=[pltpu.VMEM((ATT_TQ, s), F32), pltpu.VMEM((ATT_TQ, s), F32)],
        compiler_params=pltpu.CompilerParams(
            dimension_semantics=("arbitrary", "arbitrary", "arbitrary"),
            vmem_limit_bytes=VMEM_LIMIT),
        name=name,
    )(q, gain, *tabs, k, v)


def _post_kernel(x_ref, ya_ref, yb_ref, ga_ref, gb_ref, wa_ref, wb_ref, wout_ref,
                 gffn_ref, wrh_ref, wrl_ref, br_ref, x1_ref, h2_ref, aff_ref, aff_rows_ref):
    a = _dot(ya_ref[...], wa_ref[...])
    b = _dot(yb_ref[...], wb_ref[...])
    merged = ga_ref[...].astype(F32) * a + gb_ref[...].astype(F32) * b
    x1 = x_ref[...] + _dot(merged.astype(BF16), wout_ref[...])
    x1_ref[...] = x1
    h2 = _rms(x1, D_MODEL) * gffn_ref[...]
    h2_hi = h2.astype(BF16)
    h2_ref[...] = h2_hi
    h2_lo = (h2 - h2_hi.astype(F32)).astype(BF16)
    wrh = wrh_ref[...]
    logits = _dot(h2_hi, wrh) + _dot(h2_lo, wrh) + _dot(h2_hi, wrl_ref[...]) + br_ref[...]
    e = jnp.exp(logits - jnp.max(logits, axis=-1, keepdims=True))
    aff = e / jnp.sum(e, axis=-1, keepdims=True)
    aff_ref[...] = aff
    aff_rows_ref[...] = jnp.transpose(aff)[:N_EXPERTS]


def _post_call(x2d, ya, yb, ga, gb, wa, wb, wout, gffn, wrh, wrl, br):
    t = x2d.shape[0]
    tm = POST_TM

    def full(a):
        return pl.BlockSpec(a.shape, lambda i: (0,) * a.ndim)

    def tok(width):
        return pl.BlockSpec((tm, width), lambda i: (i, 0))

    params = (wa, wb, wout, gffn, wrh, wrl, br)
    return pl.pallas_call(
        _post_kernel,
        grid=(t // tm,),
        in_specs=[tok(D_MODEL)] * 5 + [full(a) for a in params],
        out_specs=[tok(D_MODEL), tok(D_MODEL), tok(LANES),
                   pl.BlockSpec((N_EXPERTS, tm), lambda i: (0, i))],
        out_shape=[jax.ShapeDtypeStruct((t, D_MODEL), F32),
                   jax.ShapeDtypeStruct((t, D_MODEL), BF16),
                   jax.ShapeDtypeStruct((t, LANES), F32),
                   jax.ShapeDtypeStruct((N_EXPERTS, t), F32)],
        compiler_params=pltpu.CompilerParams(
            dimension_semantics=("arbitrary",), vmem_limit_bytes=VMEM_LIMIT),
        name="post_attention",
    )(x2d, ya, yb, ga, gb, *params)


def _window_meta(start, end, cap):
    lane = lax.broadcasted_iota(I32, start.shape, 1)
    first = jnp.minimum(jnp.floor(start * (1.0 / SLOT_ALIGN)) * SLOT_ALIGN, float(cap - MOE_WIN))
    over = jnp.where((lane < N_EXPERTS) & (end - first > MOE_WIN), 1.0, 0.0)
    flag = jnp.max(over, axis=-1, keepdims=True)
    return jnp.where(lane == META_FLAG_LANE, flag, first).astype(I32)


def _route_kernel(aff_ref, aff_rows_ref, slot_t_ref, gate_t_ref, slot_r_ref, meta_ref, *, cap):
    s = aff_ref.shape[1]

    def count(mask):
        return jnp.sum(jnp.where(mask, 1.0, 0.0), axis=1, keepdims=True)

    def as_float(bits):
        return lax.bitcast_convert_type(bits, F32)

    def search(i, bits):
        cand = bits | jnp.left_shift(jnp.int32(1), 29 - i)
        return jnp.where(count(aff_rows_ref[...] >= as_float(cand)) >= cap, cand, bits)

    bits = lax.fori_loop(0, 30, search, jnp.zeros((N_EXPERTS, 1), I32))
    is_normal = bits >= MIN_NORMAL_BITS
    thr_col = jnp.where(is_normal, as_float(bits), 0.0)
    next_col = as_float(jnp.where(is_normal, bits + 1, MIN_NORMAL_BITS))
    need_col = cap - count(aff_rows_ref[...] >= next_col)

    def to_lanes(col):
        on_diag = (lax.broadcasted_iota(I32, (N_EXPERTS, LANES), 0)
                   == lax.broadcasted_iota(I32, (N_EXPERTS, LANES), 1))
        return jnp.sum(jnp.where(on_diag, col, 0.0), axis=0, keepdims=True)

    thr, thr_next, need = to_lanes(thr_col), to_lanes(next_col), to_lanes(need_col)

    blk = ROUTE_BLK
    tri = (lax.broadcasted_iota(I32, (blk, blk), 0) >= lax.broadcasted_iota(I32, (blk, blk), 1)).astype(BF16)
    eye = (lax.broadcasted_iota(I32, (LANES, LANES), 0) == lax.broadcasted_iota(I32, (LANES, LANES), 1)).astype(BF16)
    carry_eq = jnp.zeros((1, LANES), F32)
    carry_sel = jnp.zeros((1, LANES), F32)
    for c in range(s // blk):
        rows = slice(c * blk, (c + 1) * blk)
        ab = aff_ref[0, rows, :]
        gt = ab >= thr_next
        eq = jnp.where((ab >= thr) & jnp.logical_not(gt), 1.0, 0.0)
        eq_incl = _dot(tri, eq.astype(BF16)) + carry_eq
        carry_eq = eq_incl[blk - 1:blk, :]
        sel = jnp.where(gt | ((eq > 0.0) & (eq_incl - eq < need)), 1.0, 0.0)
        sel_incl = _dot(tri, sel.astype(BF16)) + carry_sel
        edge = carry_sel
        for k in range(blk // MOE_TC):
            nxt = sel_incl[(k + 1) * MOE_TC - 1:(k + 1) * MOE_TC, :]
            chunk = c * (blk // MOE_TC) + k
            meta_ref[0, chunk:chunk + 1, :] = _window_meta(edge, nxt, cap)
            edge = nxt
        carry_sel = sel_incl[blk - 1:blk, :]
        slot1 = sel * sel_incl
        slot_t_ref[0, rows, :] = slot1.astype(I32) - 1
        gate_t_ref[0, rows, :] = sel * aff_ref[0, rows, :]
        hi = jnp.floor(slot1 * (1.0 / 32.0))
        lo = slot1 - 32.0 * hi
        rows_hi = lax.dot_general(eye, hi.astype(BF16), _NT, preferred_element_type=F32)
        rows_lo = lax.dot_general(eye, lo.astype(BF16), _NT, preferred_element_type=F32)
        slot_rows = (32.0 * rows_hi + rows_lo).astype(I32) - 1
        slot_r_ref[0, :, rows] = slot_rows[:N_EXPERTS]


def _route_call(aff, aff_rows, cap):
    b, s, _ = aff.shape
    tok_spec = pl.BlockSpec((1, s, LANES), lambda bi: (bi, 0, 0))
    return pl.pallas_call(
        functools.partial(_route_kernel, cap=cap),
        grid=(b,),
        in_specs=[tok_spec, pl.BlockSpec((N_EXPERTS, s), lambda bi: (0, bi))],
        out_specs=[tok_spec, tok_spec, pl.BlockSpec((1, N_EXPERTS, s), lambda bi: (bi, 0, 0)),
                   pl.BlockSpec((1, s // MOE_TC, LANES), lambda bi: (bi, 0, 0))],
        out_shape=[jax.ShapeDtypeStruct((b, s, LANES), I32),
                   jax.ShapeDtypeStruct((b, s, LANES), F32),
                   jax.ShapeDtypeStruct((b, N_EXPERTS, s), I32),
                   jax.ShapeDtypeStruct((b, s // MOE_TC, LANES), I32)],
        compiler_params=pltpu.CompilerParams(
            dimension_semantics=("arbitrary",), vmem_limit_bytes=VMEM_LIMIT),
        name="route",
    )(aff, aff_rows)


def _meta_spec():
    return pl.BlockSpec((1, 1, 1, LANES), lambda bi, ci: (bi, ci, 0, 0), memory_space=pltpu.SMEM)


def _dispatch_kernel(meta_ref, slot_r_ref, h2_ref, xe_ref, *, cap):
    tc = h2_ref.shape[1]

    @pl.when(pl.program_id(1) == 0)
    def _():
        xe_ref[...] = jnp.zeros_like(xe_ref)

    slot_rows = slot_r_ref[0]
    h2c = h2_ref[0]

    def add_rows(first_row, rows):
        n = rows.shape[0]
        cur = xe_ref[0, pl.ds(first_row, n), :].astype(F32)
        xe_ref[0, pl.ds(first_row, n), :] = (cur + rows).astype(BF16)

    fits = meta_ref[0, 0, 0,META_FLAG_LANE] == 0

    @pl.when(fits)
    def _():
        firsts = [pl.multiple_of(meta_ref[0, 0, 0,e], SLOT_ALIGN) for e in range(N_EXPERTS)]
        win_ids = lax.broadcasted_iota(I32, (MOE_WIN, tc), 0)
        half = N_EXPERTS // 2
        for lo in (0, half):
            pick = jnp.concatenate(
                [jnp.where(win_ids == slot_rows[e:e + 1, :] - firsts[e], 1.0, 0.0).astype(BF16)
                 for e in range(lo, lo + half)], axis=0)
            got = _dot(pick, h2c)
            for i, e in enumerate(range(lo, lo + half)):
                add_rows(e * cap + firsts[e], got[i * MOE_WIN:(i + 1) * MOE_WIN])

    @pl.when(jnp.logical_not(fits))
    def _():
        slot_ids = lax.broadcasted_iota(I32, (cap, tc), 0)
        for e in range(N_EXPERTS):
            pick = jnp.where(slot_ids == slot_rows[e:e + 1, :], 1.0, 0.0).astype(BF16)
            add_rows(e * cap, _dot(pick, h2c))


def _dispatch_call(meta, slot_r, h2, cap):
    b, s, d = h2.shape
    tc = MOE_TC
    return pl.pallas_call(
        functools.partial(_dispatch_kernel, cap=cap),
        grid=(b, s // tc),
        in_specs=[_meta_spec(),
                  pl.BlockSpec((1, N_EXPERTS, tc), lambda bi, ci: (bi, 0, ci)),
                  pl.BlockSpec((1, tc, d), lambda bi, ci: (bi, ci, 0))],
        out_specs=pl.BlockSpec((1, N_EXPERTS * cap, d), lambda bi, ci: (bi, 0, 0)),
        out_shape=jax.ShapeDtypeStruct((b, N_EXPERTS * cap, d), BF16),
        compiler_params=pltpu.CompilerParams(
            dimension_semantics=("arbitrary", "arbitrary"), vmem_limit_bytes=VMEM_LIMIT),
        name="dispatch",
    )(meta, slot_r, h2)


def _expert_kernel(xe_ref, wg_ref, wu_ref, wd_ref, y_ref, wg_bf, wu_bf, wd_bf):
    @pl.when(pl.program_id(1) == 0)
    def _():
        wg_bf[...] = wg_ref[0].astype(BF16)
        wu_bf[...] = wu_ref[0].astype(BF16)
        wd_bf[...] = wd_ref[0].astype(BF16)

    nseq, cap, d = xe_ref.shape
    xe = xe_ref[...].reshape(nseq * cap, d)
    a = _dot(xe, wg_bf[...])
    u = _dot(xe, wu_bf[...])
    act = (a * jax.nn.sigmoid(a) * u).astype(BF16)
    y_ref[...] = _dot(act, wd_bf[...]).astype(BF16).reshape(nseq, cap, d)


def _expert_call(xe, wg, wu, wd, cap):
    b, _, d = xe.shape
    nseq = EXPERT_SEQS if b % EXPERT_SEQS == 0 else 1
    up_spec = pl.BlockSpec((1, d, EXPERT_FF), lambda e, bi: (e, 0, 0))
    down_spec = pl.BlockSpec((1, EXPERT_FF, d), lambda e, bi: (e, 0, 0))
    tok_spec = pl.BlockSpec((nseq, cap, d), lambda e, bi: (bi, e, 0))
    return pl.pallas_call(
        _expert_kernel,
        grid=(N_EXPERTS, b // nseq),
        in_specs=[tok_spec, up_spec, up_spec, down_spec],
        out_specs=tok_spec,
        out_shape=jax.ShapeDtypeStruct(xe.shape, BF16),
        scratch_shapes=[pltpu.VMEM((d, EXPERT_FF), BF16), pltpu.VMEM((d, EXPERT_FF), BF16),
                        pltpu.VMEM((EXPERT_FF, d), BF16)],
        compiler_params=pltpu.CompilerParams(
            dimension_semantics=("arbitrary", "arbitrary"), vmem_limit_bytes=VMEM_LIMIT),
        name="experts",
    )(xe, wg, wu, wd)


def _combine_kernel(meta_ref, x1_ref, slot_t_ref, gate_t_ref, y_ref, o_ref, ywin_ref, *, cap):
    tc = x1_ref.shape[1]
    slot_t = slot_t_ref[0]
    gate_t = gate_t_ref[0]
    fits = meta_ref[0, 0, 0,META_FLAG_LANE] == 0

    @pl.when(fits)
    def _():
        cols = N_EXPERTS * MOE_WIN
        win_shift = MOE_WIN.bit_length() - 1
        col_expert = jnp.right_shift(lax.broadcasted_iota(I32, (LANES, cols), 1), win_shift)
        spread = jnp.where(lax.broadcasted_iota(I32, (LANES, cols), 0) == col_expert, 1.0, 0.0).astype(BF16)
        slot1 = (slot_t + 1).astype(F32)
        hi = jnp.floor(slot1 * (1.0 / 32.0))
        lo = slot1 - 32.0 * hi
        slot1_cols = 32.0 * _dot(hi.astype(BF16), spread) + _dot(lo.astype(BF16), spread)
        gate_cols = _dot(gate_t.astype(BF16), spread)
        col = lax.broadcasted_iota(I32, (1, cols), 1)
        first1_cols = jnp.zeros((1, cols), F32)
        for e in range(N_EXPERTS):
            first = pl.multiple_of(meta_ref[0, 0, 0, e], SLOT_ALIGN)
            ywin_ref[e * MOE_WIN:(e + 1) * MOE_WIN, :] = y_ref[0, pl.ds(e * cap + first, MOE_WIN), :]
            first1_cols = jnp.where(jnp.right_shift(col, win_shift) == e, (first + 1).astype(F32), first1_cols)
        hit = slot1_cols - first1_cols == jnp.bitwise_and(col, MOE_WIN - 1).astype(F32)
        place = jnp.where(hit, gate_cols, 0.0).astype(BF16)
        o_ref[0] = x1_ref[0] + _dot(place, ywin_ref[...])

    @pl.when(jnp.logical_not(fits))
    def _():
        acc = x1_ref[0]
        slot_ids = lax.broadcasted_iota(I32, (tc, cap), 1)
        for e in range(N_EXPERTS):
            place = jnp.where(slot_ids == slot_t[:, e:e + 1], gate_t[:, e:e + 1], 0.0).astype(BF16)
            acc = acc + _dot(place, y_ref[0, e * cap:(e + 1) * cap, :])
        o_ref[0] = acc


def _combine_call(meta, x1, slot_t, gate_t, y, cap):
    b, s, d = x1.shape
    tc = MOE_TC
    return pl.pallas_call(
        functools.partial(_combine_kernel, cap=cap),
        grid=(b, s // tc),
        in_specs=[_meta_spec(),
                  pl.BlockSpec((1, tc, d), lambda bi, ci: (bi, ci, 0)),
                  pl.BlockSpec((1, tc, LANES), lambda bi, ci: (bi, ci, 0)),
                  pl.BlockSpec((1, tc, LANES), lambda bi, ci: (bi, ci, 0)),
                  pl.BlockSpec((1, N_EXPERTS * cap, d), lambda bi, ci: (bi, 0, 0))],
        out_specs=pl.BlockSpec((1, tc, d), lambda bi, ci: (bi, ci, 0)),
        out_shape=jax.ShapeDtypeStruct((b, s, d), F32),
        scratch_shapes=[pltpu.VMEM((N_EXPERTS * MOE_WIN, d), BF16)],
        compiler_params=pltpu.CompilerParams(
            dimension_semantics=("arbitrary", "arbitrary"), vmem_limit_bytes=VMEM_LIMIT),
        name="combine",
    )(meta, x1, slot_t, gate_t, y)


def _rope_tables(n, rot_dim, lane_lo, lane_hi, period):
    rows = n // GRID_W
    row = jnp.broadcast_to(jnp.arange(rows)[:, None], (rows, GRID_W)).reshape(n).astype(F32)
    col = jnp.broadcast_to(jnp.arange(GRID_W)[None, :], (rows, GRID_W)).reshape(n).astype(F32)
    nf = rot_dim // 4
    inv = ROPE_THETA ** (-jnp.arange(nf, dtype=F32) / nf)
    ang = jnp.concatenate([row[:, None] * inv, col[:, None] * inv], axis=-1)
    cos, sin = jnp.cos(ang), jnp.sin(ang)
    lane = np.arange(LANES)
    active = (lane >= lane_lo) & (lane < lane_hi)
    pair = np.where(active, ((lane - lane_lo) % period) // 2, 0)
    even = (lane % 2) == 0
    c = jnp.where(active[None, :], cos[:, pair], 1.0)
    sa = jnp.where((active & even)[None, :], -sin[:, pair], 0.0)
    sb = jnp.where((active & ~even)[None, :], sin[:, pair], 0.0)
    return c, sa, sb


def _pad_cols(w, heads, width):
    k = w.shape[0]
    w = w.reshape(k, heads, width)
    return jnp.pad(w, ((0, 0), (0, 0), (0, LANES - width))).reshape(k, heads * LANES)


def kernel(x, g_attn_norm, w_in, b_gate, g_q_lat, w_q_up, g_kv_lat, w_kv_up, g_mla_qnorm, g_mla_knorm, g_gqa_qnorm, g_gqa_knorm, w_mla_branch, w_gqa_branch, w_out, g_ffn_norm, w_router, b_router, w_exp_gate, w_exp_up, w_exp_down):
    b, s, d = x.shape
    assert d == D_MODEL and s % max(PRE_TM, ROUTE_BLK) == 0 and ROUTE_BLK % MOE_TC == 0
    cap = CAPACITY_FACTOR * s // N_EXPERTS
    assert cap >= MOE_WIN and cap % SLOT_ALIGN == 0
    t = b * s

    splits = np.cumsum([Q_LORA, KV_LORA, MLA_ROPE, GQA_HEADS * GQA_HEAD_DIM,
                        GQA_KV_HEADS * GQA_HEAD_DIM, GQA_KV_HEADS * GQA_HEAD_DIM, D_MODEL])
    w_ql, w_kvl, w_kr, w_qg, w_kg, w_vg, w_ga, w_gb = jnp.split(w_in, splits, axis=1)
    w_kr = jnp.pad(w_kr, ((0, 0), (MLA_NOPE, LANES - MLA_QK)))
    w_qg = _pad_cols(w_qg, GQA_HEADS, GQA_HEAD_DIM)
    w_kg = _pad_cols(w_kg, GQA_KV_HEADS, GQA_HEAD_DIM)
    w_vg = _pad_cols(w_vg, GQA_KV_HEADS, GQA_HEAD_DIM)
    win = jnp.concatenate([w_ql, w_kvl, w_kr, w_qg, w_kg, w_vg, w_ga, w_gb], axis=1).astype(BF16)
    wqup = _pad_cols(w_q_up, MLA_HEADS, MLA_QK).astype(BF16)
    wkvup = w_kv_up.astype(BF16)

    def row(v):
        return v.reshape(1, -1).astype(F32)

    gq = row(jnp.pad(g_mla_qnorm, (0, LANES - MLA_QK)))
    gk = row(jnp.pad(g_mla_knorm, (0, LANES - MLA_QK)))
    ggq = row(jnp.pad(g_gqa_qnorm, (0, LANES - GQA_HEAD_DIM)))
    ggk = row(jnp.pad(g_gqa_knorm, (0, LANES - GQA_HEAD_DIM)))

    wa = jnp.pad(w_mla_branch.reshape(MLA_HEADS, MLA_V, d), ((0, 0), (LANES - MLA_V, 0), (0, 0)))
    wa = wa.reshape(HEADS_WIDTH, d).astype(BF16)
    wb = jnp.pad(w_gqa_branch.reshape(GQA_HEADS, GQA_HEAD_DIM, d), ((0, 0), (0, LANES - GQA_HEAD_DIM), (0, 0)))
    wb = wb.reshape(HEADS_WIDTH, d).astype(BF16)

    wr = jnp.pad(w_router, ((0, 0), (0, LANES - N_EXPERTS)))
    wr_hi = wr.astype(BF16)
    wr_lo = (wr - wr_hi.astype(F32)).astype(BF16)
    br = jnp.concatenate([b_router.astype(F32), jnp.full((LANES - N_EXPERTS,), -1e30, F32)]).reshape(1, LANES)

    tabs_a = _rope_tables(s, MLA_ROPE, MLA_NOPE, MLA_QK, MLA_ROPE)
    tabs_b = _rope_tables(s, GQA_HEAD_DIM, 0, GQA_HEAD_DIM, GQA_HEAD_DIM)

    x2d = x.reshape(t, d)
    qa, ka, va, qb, kb, vb, ga, gb = _pre_call(
        x2d, s, row(g_attn_norm), win, row(g_q_lat), wqup, row(g_kv_lat), wkvup,
        gk, ggk, row(b_gate), tabs_a + tabs_b)

    def seq(a):
        return a.reshape(b, s, a.shape[-1])

    ya = _flash_call(seq(qa), gq, tabs_a, seq(ka), seq(va), 1, MLA_ONES_LANE, MLA_QK, "attention_mla")
    yb = _flash_call(seq(qb), ggq, tabs_b, seq(kb), seq(vb), GQA_GROUP, GQA_ONES_LANE, GQA_HEAD_DIM,
                     "attention_gqa")

    x1, h2, aff, aff_rows = _post_call(x2d, ya.reshape(t, HEADS_WIDTH), yb.reshape(t, HEADS_WIDTH), ga, gb,
                                       wa, wb, w_out.astype(BF16), row(g_ffn_norm), wr_hi, wr_lo, br)

    slot_t, gate_t, slot_r, meta = _route_call(seq(aff), aff_rows, cap)
    meta = meta.reshape(b, s // MOE_TC, 1, LANES)
    xe = _dispatch_call(meta, slot_r, seq(h2), cap)
    y = _expert_call(xe, w_exp_gate, w_exp_up, w_exp_down, cap)
    return _combine_call(meta, seq(x1), slot_t, gate_t, y, cap)
```

```python
import functools

import jax
import jax.numpy as jnp
import numpy as np
from jax import lax
from jax.experimental import pallas as pl
from jax.experimental.pallas import tpu as pltpu

F32 = jnp.float32
BF16 = jnp.bfloat16
I32 = jnp.int32

D_MODEL = 1024
GRID_W = 64
ROPE_THETA = 10000.0
EPS = 1e-6

MLA_HEADS = 8
MLA_NOPE = 64
MLA_ROPE = 32
MLA_QK = MLA_NOPE + MLA_ROPE
MLA_V = 64
Q_LORA = 768
KV_LORA = 256

GQA_HEADS = 8
GQA_KV_HEADS = 2
GQA_HEAD_DIM = 64
GQA_GROUP = GQA_HEADS // GQA_KV_HEADS

N_EXPERTS = 16
CAPACITY_FACTOR = 2
EXPERT_FF = 1024

MIN_NORMAL_BITS = 0x00800000
LANES = 128
HEADS_WIDTH = MLA_HEADS * LANES

_C_QLAT = 0
_C_KVLAT = _C_QLAT + Q_LORA
_C_KROPE = _C_KVLAT + KV_LORA
_C_QG = _C_KROPE + LANES
_C_KG = _C_QG + HEADS_WIDTH
_C_VG = _C_KG + GQA_KV_HEADS * LANES
_C_GA = _C_VG + 2 * GQA_KV_HEADS * LANES
_C_GB = _C_GA + D_MODEL
_C_END = _C_GB + D_MODEL

HALF = LANES // 2
LOG2_E = 1.4426950408889634

PRE_TM = 256
POST_TM = 512
EXPERT_SEQS = 2
ATT_TQ = 256
ATT_UNITS = 16
ATT_TK = 512
ATT_TK2 = 2048
ROUTE_BLK = 512
MOE_TC = 256
MOE_WIN = 64
SLOT_ALIGN = 16
META_FLAG_LANE = N_EXPERTS
VMEM_LIMIT = 56 * 1024 * 1024

_NT = (((1,), (1,)), ((), ()))


def _dot(a, b):
    return jnp.dot(a, b, preferred_element_type=F32)


def _rms(x, width):
    return x * lax.rsqrt(jnp.sum(x * x, axis=-1, keepdims=True) * (1.0 / width) + EPS)


def _value_block(x, lane, half):
    keep = (lane >= half * HALF) & (lane < (half + 1) * HALF)
    return jnp.where(lane == _ones_lane(half), 1.0, jnp.where(keep, x, 0.0))


def _ones_lane(out_half):
    return (1 - out_half) * HALF


def _rope(x, c, sa, sb):
    return x * c + pltpu.roll(x, LANES - 1, 1) * sa + pltpu.roll(x, 1, 1) * sb


def _pre_kernel(x_ref, gattn_ref, win_ref, gql_ref, wqup_ref, gkvl_ref, wkvup_ref,
                gk_ref, ggk_ref, bgate_ref,
                cm_ref, sam_ref, sbm_ref, cg_ref, sag_ref, sbg_ref,
                qa_ref, ka_ref, va_ref, qb_ref, kb_ref, vb_ref, ga_ref, gb_ref):
    x = x_ref[...]
    h = (_rms(x, D_MODEL) * gattn_ref[...]).astype(BF16)

    def proj(lo, hi):
        return _dot(h, win_ref[:, lo:hi])

    lane = lax.broadcasted_iota(I32, (x.shape[0], LANES), 1)
    low_half = lane < GQA_HEAD_DIM
    cm, sam, sbm = cm_ref[...], sam_ref[...], sbm_ref[...]
    cg, sag, sbg = cg_ref[...], sag_ref[...], sbg_ref[...]

    latents = proj(_C_QLAT, _C_QG)
    c_q = (_rms(latents[:, _C_QLAT:_C_KVLAT], Q_LORA) * gql_ref[...]).astype(BF16)
    qa_ref[...] = _dot(c_q, wqup_ref[...])

    c_kv = (_rms(latents[:, _C_KVLAT:_C_KROPE], KV_LORA) * gkvl_ref[...]).astype(BF16)
    k_rope = latents[:, _C_KROPE:_C_QG]
    kv_up = _dot(c_kv, wkvup_ref[...])
    gk = gk_ref[...]
    for hd in range(MLA_HEADS):
        kv = kv_up[:, hd * LANES:(hd + 1) * LANES]
        kseg = jnp.where(low_half, kv, 0.0) + k_rope
        kseg = _rms(kseg, MLA_QK) * gk
        ka_ref[:, hd * LANES:(hd + 1) * LANES] = _rope(kseg, cm, sam, sbm).astype(BF16)
        if hd % 2 == 0:
            kv = kv_up[:, (MLA_HEADS + hd // 2) * LANES:(MLA_HEADS + hd // 2 + 1) * LANES]
        va_ref[:, hd * LANES:(hd + 1) * LANES] = _value_block(kv, lane, hd % 2).astype(BF16)

    qb_ref[...] = proj(_C_QG, _C_KG)
    ggk = ggk_ref[...]
    kvg = proj(_C_KG, _C_GA)
    kv_w = GQA_KV_HEADS * LANES
    for hd in range(GQA_KV_HEADS):
        seg = _rms(kvg[:, hd * LANES:(hd + 1) * LANES], GQA_HEAD_DIM) * ggk
        kb_ref[:, hd * LANES:(hd + 1) * LANES] = _rope(seg, cg, sag, sbg).astype(BF16)
        for half in range(2):
            blk = 2 * hd + half
            vraw = kvg[:, kv_w + blk * LANES:kv_w + (blk + 1) * LANES]
            vb_ref[:, blk * LANES:(blk + 1) * LANES] = _value_block(vraw, lane, half).astype(BF16)

    gates = jax.nn.sigmoid(proj(_C_GA, _C_END) + bgate_ref[...])
    ga_ref[...] = gates[:, :D_MODEL].astype(BF16)
    gb_ref[...] = gates[:, D_MODEL:].astype(BF16)


def _pre_call(x2d, seq, gattn, win, gql, wqup, gkvl, wkvup, gk, ggk, bgate, tabs):
    t = x2d.shape[0]
    tm = PRE_TM
    n_seq_tiles = seq // tm

    def full(a):
        return pl.BlockSpec(a.shape, lambda i: (0,) * a.ndim)

    def tok(width):
        return pl.BlockSpec((tm, width), lambda i: (i, 0))

    tab_spec = pl.BlockSpec((tm, LANES), lambda i: (i % n_seq_tiles, 0))
    params = (gattn, win, gql, wqup, gkvl, wkvup, gk, ggk, bgate)
    kv_w = GQA_KV_HEADS * LANES
    outs = ((HEADS_WIDTH, F32), (HEADS_WIDTH, BF16), (HEADS_WIDTH, BF16), (HEADS_WIDTH, F32),
            (kv_w, BF16), (2 * kv_w, BF16), (D_MODEL, BF16), (D_MODEL, BF16))
    return pl.pallas_call(
        _pre_kernel,
        grid=(t // tm,),
        in_specs=[tok(D_MODEL)] + [full(a) for a in params] + [tab_spec] * 6,
        out_specs=[tok(w) for w, _ in outs],
        out_shape=[jax.ShapeDtypeStruct((t, w), dt) for w, dt in outs],
        compiler_params=pltpu.CompilerParams(
            dimension_semantics=("arbitrary",), vmem_limit_bytes=VMEM_LIMIT),
        name="pre_attention",
    )(x2d, *params, *tabs)


def _flash_kernel(q_ref, gain_ref, c_ref, sa_ref, sb_ref, k_ref, v_ref, o_ref, s0_ref, s1_ref, *,
                  units, k_block, v_block, head_dim):
    tq = s0_ref.shape[0]
    s_len = k_ref.shape[1]
    s_bufs = (s0_ref, s1_ref)
    q_scale = head_dim ** -0.5 * LOG2_E
    lane = lax.broadcasted_iota(I32, (tq, LANES), 1)

    def lanes_of(block):
        return slice(block * LANES, (block + 1) * LANES)

    def q_of(u):
        r, g = units[u]
        rows = slice(r * tq, (r + 1) * tq)
        q = _rms(q_ref[0, rows, lanes_of(g)], head_dim) * gain_ref[...]
        q = _rope(q, c_ref[rows, :], sa_ref[rows, :], sb_ref[rows, :]) * q_scale
        return q.astype(BF16)

    def score_pass(u, j, q, m128):
        cols = slice(j * ATT_TK, (j + 1) * ATT_TK)
        keys = k_ref[0, cols, lanes_of(k_block[units[u][1]])]
        s = lax.dot_general(q, keys, _NT, preferred_element_type=F32)
        s_bufs[u % 2][:, cols] = s
        for c in range(ATT_TK // LANES):
            m128 = jnp.maximum(m128, s[:, c * LANES:(c + 1) * LANES])
        return m128

    def value_pass(u, j, m, acc):
        cols = slice(j * ATT_TK2, (j + 1) * ATT_TK2)
        p = jnp.exp2(s_bufs[u % 2][:, cols] - m)
        return acc + _dot(p.astype(BF16), v_ref[0, cols, lanes_of(v_block[units[u][1]])])

    neg = jnp.full((tq, LANES), -1e30, F32)
    m128 = neg
    q_first = q_of(0)
    for j in range(s_len // ATT_TK):
        m128 = score_pass(0, j, q_first, m128)
    ratio = ATT_TK2 // ATT_TK
    for u in range(len(units)):
        m = jnp.max(m128, axis=-1, keepdims=True)
        acc = jnp.zeros((tq, LANES), F32)
        has_next = u + 1 < len(units)
        if has_next:
            q_next = q_of(u + 1)
            m128 = neg
        for j2 in range(s_len // ATT_TK2):
            acc = value_pass(u, j2, m, acc)
            if has_next:
                for j in range(j2 * ratio, (j2 + 1) * ratio):
                    m128 = score_pass(u + 1, j, q_next, m128)
        r, g = units[u]
        ones = _ones_lane(g % 2)
        out = acc / acc[:, ones:ones + 1]
        if g % 2 == 0:
            out_even = out
        else:
            pair = jnp.where(lane < HALF, out_even, out)
            o_ref[0, r * tq:(r + 1) * tq, lanes_of(g // 2)] = pair.astype(BF16)


def _flash_call(q, gain, tabs, k, v, heads, k_block, v_block, head_dim, name):
    b, s, qw = q.shape
    n_steps = qw // (heads * LANES)
    kw = k.shape[2] // n_steps
    vw = v.shape[2] // n_steps
    assert heads % 2 == 0 and ATT_UNITS % heads == 0 and len(k_block) == len(v_block) == heads
    assert s % ATT_TK2 == 0 and ATT_TK2 % ATT_TK == 0
    row_tiles = ATT_UNITS // heads
    rows = ATT_TQ * row_tiles
    assert s % rows == 0
    units = tuple((r, g) for r in range(row_tiles) for g in range(heads))
    q_spec = pl.BlockSpec((1, rows, heads * LANES), lambda bi, kh, qi: (bi, qi, kh))
    o_spec = pl.BlockSpec((1, rows, heads * HALF), lambda bi, kh, qi: (bi, qi, kh))
    k_spec = pl.BlockSpec((1, s, kw), lambda bi, kh, qi: (bi, 0, kh))
    v_spec = pl.BlockSpec((1, s, vw), lambda bi, kh, qi: (bi, 0, kh))
    tab_spec = pl.BlockSpec((rows, LANES), lambda bi, kh, qi: (qi, 0))
    gain_spec = pl.BlockSpec((1, LANES), lambda bi, kh, qi: (0, 0))
    return pl.pallas_call(
        functools.partial(_flash_kernel, units=units, k_block=k_block, v_block=v_block, head_dim=head_dim),
        grid=(b, n_steps, s // rows),
        in_specs=[q_spec, gain_spec, tab_spec, tab_spec, tab_spec, k_spec, v_spec],
        out_specs=o_spec,
        out_shape=jax.ShapeDtypeStruct((b, s, qw // 2), BF16),
        scratch_shapes=[pltpu.VMEM((ATT_TQ, s), F32), pltpu.VMEM((ATT_TQ, s), F32)],
        compiler_params=pltpu.CompilerParams(
            dimension_semantics=("arbitrary", "arbitrary", "arbitrary"),
            vmem_limit_bytes=VMEM_LIMIT),
        name=name,
    )(q, gain, *tabs, k, v)


def _post_kernel(x_ref, ya_ref, yb_ref, ga_ref, gb_ref, wa_ref, wb_ref, wout_ref,
                 gffn_ref, wrh_ref, wrl_ref, br_ref, x1_ref, h2_ref, aff_ref, aff_rows_ref):
    a = _dot(ya_ref[...], wa_ref[...])
    b = _dot(yb_ref[...], wb_ref[...])
    merged = ga_ref[...].astype(F32) * a + gb_ref[...].astype(F32) * b
    x1 = x_ref[...] + _dot(merged.astype(BF16), wout_ref[...])
    x1_ref[...] = x1
    h2 = _rms(x1, D_MODEL) * gffn_ref[...]
    h2_hi = h2.astype(BF16)
    h2_ref[...] = h2_hi
    h2_lo = (h2 - h2_hi.astype(F32)).astype(BF16)
    wrh = wrh_ref[...]
    logits = _dot(h2_hi, wrh) + _dot(h2_lo, wrh) + _dot(h2_hi, wrl_ref[...]) + br_ref[...]
    e = jnp.exp(logits - jnp.max(logits, axis=-1, keepdims=True))
    aff = e / jnp.sum(e, axis=-1, keepdims=True)
    aff_ref[...] = aff
    aff_rows_ref[...] = jnp.transpose(aff)[:N_EXPERTS]


def _post_call(x2d, ya, yb, ga, gb, wa, wb, wout, gffn, wrh, wrl, br):
    t = x2d.shape[0]
    tm = POST_TM

    def full(a):
        return pl.BlockSpec(a.shape, lambda i: (0,) * a.ndim)

    def tok(width):
        return pl.BlockSpec((tm, width), lambda i: (i, 0))

    params = (wa, wb, wout, gffn, wrh, wrl, br)
    return pl.pallas_call(
        _post_kernel,
        grid=(t // tm,),
        in_specs=[tok(a.shape[1]) for a in (x2d, ya, yb, ga, gb)] + [full(a) for a in params],
        out_specs=[tok(D_MODEL), tok(D_MODEL), tok(LANES),
                   pl.BlockSpec((N_EXPERTS, tm), lambda i: (0, i))],
        out_shape=[jax.ShapeDtypeStruct((t, D_MODEL), F32),
                   jax.ShapeDtypeStruct((t, D_MODEL), BF16),
                   jax.ShapeDtypeStruct((t, LANES), F32),
                   jax.ShapeDtypeStruct((N_EXPERTS, t), F32)],
        compiler_params=pltpu.CompilerParams(
            dimension_semantics=("arbitrary",), vmem_limit_bytes=VMEM_LIMIT),
        name="post_attention",
    )(x2d, ya, yb, ga, gb, *params)


def _window_meta(start, end, cap):
    lane = lax.broadcasted_iota(I32, start.shape, 1)
    first = jnp.minimum(jnp.floor(start * (1.0 / SLOT_ALIGN)) * SLOT_ALIGN, float(cap - MOE_WIN))
    over = jnp.where((lane < N_EXPERTS) & (end - first > MOE_WIN), 1.0, 0.0)
    flag = jnp.max(over, axis=-1, keepdims=True)
    return jnp.where(lane == META_FLAG_LANE, flag, first).astype(I32)


def _route_kernel(aff_ref, aff_rows_ref, slot_t_ref, gate_t_ref, slot_r_ref, meta_ref, *, cap):
    s = aff_ref.shape[1]

    def count(mask):
        return jnp.sum(jnp.where(mask, 1.0, 0.0), axis=1, keepdims=True)

    def as_float(bits):
        return lax.bitcast_convert_type(bits, F32)

    def search(i, bits):
        cand = bits | jnp.left_shift(jnp.int32(1), 29 - i)
        return jnp.where(count(aff_rows_ref[...] >= as_float(cand)) >= cap, cand, bits)

    bits = lax.fori_loop(0, 30, search, jnp.zeros((N_EXPERTS, 1), I32))
    is_normal = bits >= MIN_NORMAL_BITS
    thr_col = jnp.where(is_normal, as_float(bits), 0.0)
    next_col = as_float(jnp.where(is_normal, bits + 1, MIN_NORMAL_BITS))
    need_col = cap - count(aff_rows_ref[...] >= next_col)

    def to_lanes(col):
        on_diag = (lax.broadcasted_iota(I32, (N_EXPERTS, LANES), 0)
                   == lax.broadcasted_iota(I32, (N_EXPERTS, LANES), 1))
        return jnp.sum(jnp.where(on_diag, col, 0.0), axis=0, keepdims=True)

    thr, thr_next, need = to_lanes(thr_col), to_lanes(next_col), to_lanes(need_col)

    blk = ROUTE_BLK
    tri = (lax.broadcasted_iota(I32, (blk, blk), 0) >= lax.broadcasted_iota(I32, (blk, blk), 1)).astype(BF16)
    eye = (lax.broadcasted_iota(I32, (LANES, LANES), 0) == lax.broadcasted_iota(I32, (LANES, LANES), 1)).astype(BF16)
    carry_eq = jnp.zeros((1, LANES), F32)
    carry_sel = jnp.zeros((1, LANES), F32)
    for c in range(s // blk):
        rows = slice(c * blk, (c + 1) * blk)
        ab = aff_ref[0, rows, :]
        gt = ab >= thr_next
        eq = jnp.where((ab >= thr) & jnp.logical_not(gt), 1.0, 0.0)
        eq_incl = _dot(tri, eq.astype(BF16)) + carry_eq
        carry_eq = eq_incl[blk - 1:blk, :]
        sel = jnp.where(gt | ((eq > 0.0) & (eq_incl - eq < need)), 1.0, 0.0)
        sel_incl = _dot(tri, sel.astype(BF16)) + carry_sel
        edge = carry_sel
        for k in range(blk // MOE_TC):
            nxt = sel_incl[(k + 1) * MOE_TC - 1:(k + 1) * MOE_TC, :]
            chunk = c * (blk // MOE_TC) + k
            meta_ref[0, chunk:chunk + 1, :] = _window_meta(edge, nxt, cap)
            edge = nxt
        carry_sel = sel_incl[blk - 1:blk, :]
        slot1 = sel * sel_incl
        slot_t_ref[0, rows, :] = slot1.astype(I32) - 1
        gate_t_ref[0, rows, :] = sel * aff_ref[0, rows, :]
        hi = jnp.floor(slot1 * (1.0 / 32.0))
        lo = slot1 - 32.0 * hi
        rows_hi = lax.dot_general(eye, hi.astype(BF16), _NT, preferred_element_type=F32)
        rows_lo = lax.dot_general(eye, lo.astype(BF16), _NT, preferred_element_type=F32)
        slot_rows = (32.0 * rows_hi + rows_lo).astype(I32) - 1
        slot_r_ref[0, :, rows] = slot_rows[:N_EXPERTS]


def _route_call(aff, aff_rows, cap):
    b, s, _ = aff.shape
    tok_spec = pl.BlockSpec((1, s, LANES), lambda bi: (bi, 0, 0))
    return pl.pallas_call(
        functools.partial(_route_kernel, cap=cap),
        grid=(b,),
        in_specs=[tok_spec, pl.BlockSpec((N_EXPERTS, s), lambda bi: (0, bi))],
        out_specs=[tok_spec, tok_spec, pl.BlockSpec((1, N_EXPERTS, s), lambda bi: (bi, 0, 0)),
                   pl.BlockSpec((1, s // MOE_TC, LANES), lambda bi: (bi, 0, 0))],
        out_shape=[jax.ShapeDtypeStruct((b, s, LANES), I32),
                   jax.ShapeDtypeStruct((b, s, LANES), F32),
                   jax.ShapeDtypeStruct((b, N_EXPERTS, s), I32),
                   jax.ShapeDtypeStruct((b, s // MOE_TC, LANES), I32)],
        compiler_params=pltpu.CompilerParams(
            dimension_semantics=("arbitrary",), vmem_limit_bytes=VMEM_LIMIT),
        name="route",
    )(aff, aff_rows)


def _meta_spec():
    return pl.BlockSpec((1, 1, 1, LANES), lambda bi, ci: (bi, ci, 0, 0), memory_space=pltpu.SMEM)


def _dispatch_kernel(meta_ref, slot_r_ref, h2_ref, xe_ref, *, cap):
    tc = h2_ref.shape[1]

    @pl.when(pl.program_id(1) == 0)
    def _():
        xe_ref[...] = jnp.zeros_like(xe_ref)

    slot_rows = slot_r_ref[0]
    h2c = h2_ref[0]

    def add_rows(first_row, rows):
        n = rows.shape[0]
        cur = xe_ref[0, pl.ds(first_row, n), :].astype(F32)
        xe_ref[0, pl.ds(first_row, n), :] = (cur + rows).astype(BF16)

    fits = meta_ref[0, 0, 0,META_FLAG_LANE] == 0

    @pl.when(fits)
    def _():
        firsts = [pl.multiple_of(meta_ref[0, 0, 0,e], SLOT_ALIGN) for e in range(N_EXPERTS)]
        win_ids = lax.broadcasted_iota(I32, (MOE_WIN, tc), 0)
        half = N_EXPERTS // 2
        for lo in (0, half):
            pick = jnp.concatenate(
                [jnp.where(win_ids == slot_rows[e:e + 1, :] - firsts[e], 1.0, 0.0).astype(BF16)
                 for e in range(lo, lo + half)], axis=0)
            got = _dot(pick, h2c)
            for i, e in enumerate(range(lo, lo + half)):
                add_rows(e * cap + firsts[e], got[i * MOE_WIN:(i + 1) * MOE_WIN])

    @pl.when(jnp.logical_not(fits))
    def _():
        slot_ids = lax.broadcasted_iota(I32, (cap, tc), 0)
        for e in range(N_EXPERTS):
            pick = jnp.where(slot_ids == slot_rows[e:e + 1, :], 1.0, 0.0).astype(BF16)
            add_rows(e * cap, _dot(pick, h2c))


def _dispatch_call(meta, slot_r, h2, cap):
    b, s, d = h2.shape
    tc = MOE_TC
    return pl.pallas_call(
        functools.partial(_dispatch_kernel, cap=cap),
        grid=(b, s // tc),
        in_specs=[_meta_spec(),
                  pl.BlockSpec((1, N_EXPERTS, tc), lambda bi, ci: (bi, 0, ci)),
                  pl.BlockSpec((1, tc, d), lambda bi, ci: (bi, ci, 0))],
        out_specs=pl.BlockSpec((1, N_EXPERTS * cap, d), lambda bi, ci: (bi, 0, 0)),
        out_shape=jax.ShapeDtypeStruct((b, N_EXPERTS * cap, d), BF16),
        compiler_params=pltpu.CompilerParams(
            dimension_semantics=("arbitrary", "arbitrary"), vmem_limit_bytes=VMEM_LIMIT),
        name="dispatch",
    )(meta, slot_r, h2)


def _expert_kernel(xe_ref, wg_ref, wu_ref, wd_ref, y_ref, wg_bf, wu_bf, wd_bf):
    @pl.when(pl.program_id(1) == 0)
    def _():
        wg_bf[...] = wg_ref[0].astype(BF16)
        wu_bf[...] = wu_ref[0].astype(BF16)
        wd_bf[...] = wd_ref[0].astype(BF16)

    nseq, cap, d = xe_ref.shape
    xe = xe_ref[...].reshape(nseq * cap, d)
    a = _dot(xe, wg_bf[...])
    u = _dot(xe, wu_bf[...])
    act = (a * jax.nn.sigmoid(a) * u).astype(BF16)
    y_ref[...] = _dot(act, wd_bf[...]).astype(BF16).reshape(nseq, cap, d)


def _expert_call(xe, wg, wu, wd, cap):
    b, _, d = xe.shape
    nseq = EXPERT_SEQS if b % EXPERT_SEQS == 0 else 1
    up_spec = pl.BlockSpec((1, d, EXPERT_FF), lambda e, bi: (e, 0, 0))
    down_spec = pl.BlockSpec((1, EXPERT_FF, d), lambda e, bi: (e, 0, 0))
    tok_spec = pl.BlockSpec((nseq, cap, d), lambda e, bi: (bi, e, 0))
    return pl.pallas_call(
        _expert_kernel,
        grid=(N_EXPERTS, b // nseq),
        in_specs=[tok_spec, up_spec, up_spec, down_spec],
        out_specs=tok_spec,
        out_shape=jax.ShapeDtypeStruct(xe.shape, BF16),
        scratch_shapes=[pltpu.VMEM((d, EXPERT_FF), BF16), pltpu.VMEM((d, EXPERT_FF), BF16),
                        pltpu.VMEM((EXPERT_FF, d), BF16)],
        compiler_params=pltpu.CompilerParams(
            dimension_semantics=("arbitrary", "arbitrary"), vmem_limit_bytes=VMEM_LIMIT),
        name="experts",
    )(xe, wg, wu, wd)


def _combine_kernel(meta_ref, x1_ref, slot_t_ref, gate_t_ref, y_ref, o_ref, ywin_ref, *, cap):
    tc = x1_ref.shape[1]
    slot_t = slot_t_ref[0]
    gate_t = gate_t_ref[0]
    fits = meta_ref[0, 0, 0,META_FLAG_LANE] == 0

    @pl.when(fits)
    def _():
        cols = N_EXPERTS * MOE_WIN
        win_shift = MOE_WIN.bit_length() - 1
        col_expert = jnp.right_shift(lax.broadcasted_iota(I32, (LANES, cols), 1), win_shift)
        spread = jnp.where(lax.broadcasted_iota(I32, (LANES, cols), 0) == col_expert, 1.0, 0.0).astype(BF16)
        slot1 = (slot_t + 1).astype(F32)
        hi = jnp.floor(slot1 * (1.0 / 32.0))
        lo = slot1 - 32.0 * hi
        slot1_cols = 32.0 * _dot(hi.astype(BF16), spread) + _dot(lo.astype(BF16), spread)
        gate_cols = _dot(gate_t.astype(BF16), spread)
        col = lax.broadcasted_iota(I32, (1, cols), 1)
        first1_cols = jnp.zeros((1, cols), F32)
        for e in range(N_EXPERTS):
            first = pl.multiple_of(meta_ref[0, 0, 0, e], SLOT_ALIGN)
            ywin_ref[e * MOE_WIN:(e + 1) * MOE_WIN, :] = y_ref[0, pl.ds(e * cap + first, MOE_WIN), :]
            first1_cols = jnp.where(jnp.right_shift(col, win_shift) == e, (first + 1).astype(F32), first1_cols)
        hit = slot1_cols - first1_cols == jnp.bitwise_and(col, MOE_WIN - 1).astype(F32)
        place = jnp.where(hit, gate_cols, 0.0).astype(BF16)
        o_ref[0] = x1_ref[0] + _dot(place, ywin_ref[...])

    @pl.when(jnp.logical_not(fits))
    def _():
        acc = x1_ref[0]
        slot_ids = lax.broadcasted_iota(I32, (tc, cap), 1)
        for e in range(N_EXPERTS):
            place = jnp.where(slot_ids == slot_t[:, e:e + 1], gate_t[:, e:e + 1], 0.0).astype(BF16)
            acc = acc + _dot(place, y_ref[0, e * cap:(e + 1) * cap, :])
        o_ref[0] = acc


def _combine_call(meta, x1, slot_t, gate_t, y, cap):
    b, s, d = x1.shape
    tc = MOE_TC
    return pl.pallas_call(
        functools.partial(_combine_kernel, cap=cap),
        grid=(b, s // tc),
        in_specs=[_meta_spec(),
                  pl.BlockSpec((1, tc, d), lambda bi, ci: (bi, ci, 0)),
                  pl.BlockSpec((1, tc, LANES), lambda bi, ci: (bi, ci, 0)),
                  pl.BlockSpec((1, tc, LANES), lambda bi, ci: (bi, ci, 0)),
                  pl.BlockSpec((1, N_EXPERTS * cap, d), lambda bi, ci: (bi, 0, 0))],
        out_specs=pl.BlockSpec((1, tc, d), lambda bi, ci: (bi, ci, 0)),
        out_shape=jax.ShapeDtypeStruct((b, s, d), F32),
        scratch_shapes=[pltpu.VMEM((N_EXPERTS * MOE_WIN, d), BF16)],
        compiler_params=pltpu.CompilerParams(
            dimension_semantics=("arbitrary", "arbitrary"), vmem_limit_bytes=VMEM_LIMIT),
        name="combine",
    )(meta, x1, slot_t, gate_t, y)


def _rope_tables(n, rot_dim, lane_lo, lane_hi, period):
    rows = n // GRID_W
    row = jnp.broadcast_to(jnp.arange(rows)[:, None], (rows, GRID_W)).reshape(n).astype(F32)
    col = jnp.broadcast_to(jnp.arange(GRID_W)[None, :], (rows, GRID_W)).reshape(n).astype(F32)
    nf = rot_dim // 4
    inv = ROPE_THETA ** (-jnp.arange(nf, dtype=F32) / nf)
    ang = jnp.concatenate([row[:, None] * inv, col[:, None] * inv], axis=-1)
    cos, sin = jnp.cos(ang), jnp.sin(ang)
    lane = np.arange(LANES)
    active = (lane >= lane_lo) & (lane < lane_hi)
    pair = np.where(active, ((lane - lane_lo) % period) // 2, 0)
    even = (lane % 2) == 0
    c = jnp.where(active[None, :], cos[:, pair], 1.0)
    sa = jnp.where((active & even)[None, :], -sin[:, pair], 0.0)
    sb = jnp.where((active & ~even)[None, :], sin[:, pair], 0.0)
    return c, sa, sb


def _pad_cols(w, heads, width):
    k = w.shape[0]
    w = w.reshape(k, heads, width)
    return jnp.pad(w, ((0, 0), (0, 0), (0, LANES - width))).reshape(k, heads * LANES)


def kernel(x, g_attn_norm, w_in, b_gate, g_q_lat, w_q_up, g_kv_lat, w_kv_up, g_mla_qnorm, g_mla_knorm, g_gqa_qnorm, g_gqa_knorm, w_mla_branch, w_gqa_branch, w_out, g_ffn_norm, w_router, b_router, w_exp_gate, w_exp_up, w_exp_down):
    b, s, d = x.shape
    assert d == D_MODEL and s % max(PRE_TM, ROUTE_BLK) == 0 and ROUTE_BLK % MOE_TC == 0
    cap = CAPACITY_FACTOR * s // N_EXPERTS
    assert cap >= MOE_WIN and cap % SLOT_ALIGN == 0
    t = b * s

    splits = np.cumsum([Q_LORA, KV_LORA, MLA_ROPE, GQA_HEADS * GQA_HEAD_DIM,
                        GQA_KV_HEADS * GQA_HEAD_DIM, GQA_KV_HEADS * GQA_HEAD_DIM, D_MODEL])
    w_ql, w_kvl, w_kr, w_qg, w_kg, w_vg, w_ga, w_gb = jnp.split(w_in, splits, axis=1)
    w_kr = jnp.pad(w_kr, ((0, 0), (MLA_NOPE, LANES - MLA_QK)))
    w_qg = _pad_cols(w_qg, GQA_HEADS, GQA_HEAD_DIM)
    w_kg = _pad_cols(w_kg, GQA_KV_HEADS, GQA_HEAD_DIM)
    w_vg = w_vg.reshape(d, GQA_KV_HEADS, 1, GQA_HEAD_DIM)
    zeros = jnp.zeros_like(w_vg)
    w_vg = jnp.concatenate([jnp.concatenate([w_vg, zeros], axis=3),
                            jnp.concatenate([zeros, w_vg], axis=3)], axis=2).reshape(d, -1)
    win = jnp.concatenate([w_ql, w_kvl, w_kr, w_qg, w_kg, w_vg, w_ga, w_gb], axis=1).astype(BF16)
    assert win.shape[1] == _C_END
    wqup = _pad_cols(w_q_up, MLA_HEADS, MLA_QK).astype(BF16)
    v_even = w_kv_up.reshape(KV_LORA, MLA_HEADS, LANES)[:, 0::2, MLA_NOPE:]
    wkvup = jnp.concatenate([w_kv_up, _pad_cols(v_even.reshape(KV_LORA, -1), MLA_HEADS // 2, MLA_V)],
                            axis=1).astype(BF16)

    def row(v):
        return v.reshape(1, -1).astype(F32)

    gq = row(jnp.pad(g_mla_qnorm, (0, LANES - MLA_QK)))
    gk = row(jnp.pad(g_mla_knorm, (0, LANES - MLA_QK)))
    ggq = row(jnp.pad(g_gqa_qnorm, (0, LANES - GQA_HEAD_DIM)))
    ggk = row(jnp.pad(g_gqa_knorm, (0, LANES - GQA_HEAD_DIM)))

    assert MLA_V == HALF and GQA_HEAD_DIM == HALF
    wa = w_mla_branch.astype(BF16)
    wb = w_gqa_branch.astype(BF16)

    wr = jnp.pad(w_router, ((0, 0), (0, LANES - N_EXPERTS)))
    wr_hi = wr.astype(BF16)
    wr_lo = (wr - wr_hi.astype(F32)).astype(BF16)
    br = jnp.concatenate([b_router.astype(F32), jnp.full((LANES - N_EXPERTS,), -1e30, F32)]).reshape(1, LANES)

    tabs_a = _rope_tables(s, MLA_ROPE, MLA_NOPE, MLA_QK, MLA_ROPE)
    tabs_b = _rope_tables(s, GQA_HEAD_DIM, 0, GQA_HEAD_DIM, GQA_HEAD_DIM)

    x2d = x.reshape(t, d)
    qa, ka, va, qb, kb, vb, ga, gb = _pre_call(
        x2d, s, row(g_attn_norm), win, row(g_q_lat), wqup, row(g_kv_lat), wkvup,
        gk, ggk, row(b_gate), tabs_a + tabs_b)

    def seq(a):
        return a.reshape(b, s, a.shape[-1])

    ya = _flash_call(seq(qa), gq, tabs_a, seq(ka), seq(va), 2, (0, 1), (0, 1), MLA_QK, "attention_mla")
    yb = _flash_call(seq(qb), ggq, tabs_b, seq(kb), seq(vb), GQA_GROUP, (0,) * GQA_GROUP,
                     tuple(g % 2 for g in range(GQA_GROUP)), GQA_HEAD_DIM, "attention_gqa")

    x1, h2, aff, aff_rows = _post_call(x2d, ya.reshape(t, -1), yb.reshape(t, -1), ga, gb,
                                       wa, wb, w_out.astype(BF16), row(g_ffn_norm), wr_hi, wr_lo, br)

    slot_t, gate_t, slot_r, meta = _route_call(seq(aff), aff_rows, cap)
    meta = meta.reshape(b, s // MOE_TC, 1, LANES)
    xe = _dispatch_call(meta, slot_r, seq(h2), cap)
    y = _expert_call(xe, w_exp_gate, w_exp_up, w_exp_down, cap)
    return _combine_call(meta, seq(x1), slot_t, gate_t, y, cap)
```

```python
import functools

import jax
import jax.numpy as jnp
import numpy as np
from jax import lax
from jax.experimental import pallas as pl
from jax.experimental.pallas import tpu as pltpu

F32 = jnp.float32
BF16 = jnp.bfloat16
I32 = jnp.int32

D_MODEL = 1024
GRID_W = 64
ROPE_THETA = 10000.0
EPS = 1e-6

MLA_HEADS = 8
MLA_NOPE = 64
MLA_ROPE = 32
MLA_QK = MLA_NOPE + MLA_ROPE
MLA_V = 64
Q_LORA = 768
KV_LORA = 256

GQA_HEADS = 8
GQA_KV_HEADS = 2
GQA_HEAD_DIM = 64
GQA_GROUP = GQA_HEADS // GQA_KV_HEADS

N_EXPERTS = 16
CAPACITY_FACTOR = 2
EXPERT_FF = 1024

MIN_NORMAL_BITS = 0x00800000
LANES = 128
HEADS_WIDTH = MLA_HEADS * LANES

_C_QLAT = 0
_C_KVLAT = _C_QLAT + Q_LORA
_C_KROPE = _C_KVLAT + KV_LORA
_C_QG = _C_KROPE + LANES
_C_KG = _C_QG + HEADS_WIDTH
_C_VG = _C_KG + GQA_KV_HEADS * LANES
_C_GA = _C_VG + 2 * GQA_KV_HEADS * LANES
_C_GB = _C_GA + D_MODEL
_C_END = _C_GB + D_MODEL

HALF = LANES // 2
LOG2_E = 1.4426950408889634

PRE_TM = 256
POST_TM = 512
EXPERT_SEQS = 2
ATT_TQ = 256
ATT_UNITS = 16
ATT_TK = 512
ATT_TK2 = 2048
ROUTE_BLK = 512
MOE_TC = 256
MOE_STEP_CHUNKS = 2
MOE_WIN = 64
SLOT_ALIGN = 16
META_FLAG_LANE = N_EXPERTS
VMEM_LIMIT = 56 * 1024 * 1024

_NT = (((1,), (1,)), ((), ()))


def _dot(a, b):
    return jnp.dot(a, b, preferred_element_type=F32)


def _rms(x, width):
    return x * lax.rsqrt(jnp.sum(x * x, axis=-1, keepdims=True) * (1.0 / width) + EPS)


def _value_block(x, lane, half):
    keep = (lane >= half * HALF) & (lane < (half + 1) * HALF)
    return jnp.where(lane == _ones_lane(half), 1.0, jnp.where(keep, x, 0.0))


def _ones_lane(out_half):
    return (1 - out_half) * HALF


def _rope(x, c, sa, sb):
    return x * c + pltpu.roll(x, LANES - 1, 1) * sa + pltpu.roll(x, 1, 1) * sb


def _pre_kernel(x_ref, gattn_ref, win_ref, gql_ref, wqup_ref, gkvl_ref, wkvup_ref,
                gk_ref, ggk_ref, bgate_ref,
                cm_ref, sam_ref, sbm_ref, cg_ref, sag_ref, sbg_ref,
                qa_ref, ka_ref, va_ref, qb_ref, kb_ref, vb_ref, ga_ref, gb_ref):
    x = x_ref[...]
    h = (_rms(x, D_MODEL) * gattn_ref[...]).astype(BF16)

    def proj(lo, hi):
        return _dot(h, win_ref[:, lo:hi])

    lane = lax.broadcasted_iota(I32, (x.shape[0], LANES), 1)
    low_half = lane < GQA_HEAD_DIM
    cm, sam, sbm = cm_ref[...], sam_ref[...], sbm_ref[...]
    cg, sag, sbg = cg_ref[...], sag_ref[...], sbg_ref[...]

    latents = proj(_C_QLAT, _C_QG)
    c_q = (_rms(latents[:, _C_QLAT:_C_KVLAT], Q_LORA) * gql_ref[...]).astype(BF16)
    qa_ref[...] = _dot(c_q, wqup_ref[...])

    c_kv = (_rms(latents[:, _C_KVLAT:_C_KROPE], KV_LORA) * gkvl_ref[...]).astype(BF16)
    k_rope = latents[:, _C_KROPE:_C_QG]
    kv_up = _dot(c_kv, wkvup_ref[...])
    gk = gk_ref[...]
    for hd in range(MLA_HEADS):
        kv = kv_up[:, hd * LANES:(hd + 1) * LANES]
        kseg = jnp.where(low_half, kv, 0.0) + k_rope
        kseg = _rms(kseg, MLA_QK) * gk
        ka_ref[:, hd * LANES:(hd + 1) * LANES] = _rope(kseg, cm, sam, sbm).astype(BF16)
        if hd % 2 == 0:
            kv = kv_up[:, (MLA_HEADS + hd // 2) * LANES:(MLA_HEADS + hd // 2 + 1) * LANES]
        va_ref[:, hd * LANES:(hd + 1) * LANES] = _value_block(kv, lane, hd % 2).astype(BF16)

    qb_ref[...] = proj(_C_QG, _C_KG)
    ggk = ggk_ref[...]
    kvg = proj(_C_KG, _C_GA)
    kv_w = GQA_KV_HEADS * LANES
    for hd in range(GQA_KV_HEADS):
        seg = _rms(kvg[:, hd * LANES:(hd + 1) * LANES], GQA_HEAD_DIM) * ggk
        kb_ref[:, hd * LANES:(hd + 1) * LANES] = _rope(seg, cg, sag, sbg).astype(BF16)
        for half in range(2):
            blk = 2 * hd + half
            vraw = kvg[:, kv_w + blk * LANES:kv_w + (blk + 1) * LANES]
            vb_ref[:, blk * LANES:(blk + 1) * LANES] = _value_block(vraw, lane, half).astype(BF16)

    gates = jax.nn.sigmoid(proj(_C_GA, _C_END) + bgate_ref[...])
    ga_ref[...] = gates[:, :D_MODEL].astype(BF16)
    gb_ref[...] = gates[:, D_MODEL:].astype(BF16)


def _pre_call(x2d, seq, gattn, win, gql, wqup, gkvl, wkvup, gk, ggk, bgate, tabs):
    t = x2d.shape[0]
    tm = PRE_TM
    n_seq_tiles = seq // tm

    def full(a):
        return pl.BlockSpec(a.shape, lambda i: (0,) * a.ndim)

    def tok(width):
        return pl.BlockSpec((tm, width), lambda i: (i, 0))

    tab_spec = pl.BlockSpec((tm, LANES), lambda i: (i % n_seq_tiles, 0))
    params = (gattn, win, gql, wqup, gkvl, wkvup, gk, ggk, bgate)
    kv_w = GQA_KV_HEADS * LANES
    outs = ((HEADS_WIDTH, F32), (HEADS_WIDTH, BF16), (HEADS_WIDTH, BF16), (HEADS_WIDTH, F32),
            (kv_w, BF16), (2 * kv_w, BF16), (D_MODEL, BF16), (D_MODEL, BF16))
    return pl.pallas_call(
        _pre_kernel,
        grid=(t // tm,),
        in_specs=[tok(D_MODEL)] + [full(a) for a in params] + [tab_spec] * 6,
        out_specs=[tok(w) for w, _ in outs],
        out_shape=[jax.ShapeDtypeStruct((t, w), dt) for w, dt in outs],
        compiler_params=pltpu.CompilerParams(
            dimension_semantics=("arbitrary",), vmem_limit_bytes=VMEM_LIMIT),
        name="pre_attention",
    )(x2d, *params, *tabs)


def _flash_kernel(q_ref, gain_ref, c_ref, sa_ref, sb_ref, k_ref, v_ref, o_ref, s0_ref, s1_ref, *,
                  units, k_block, v_block, head_dim):
    tq = s0_ref.shape[0]
    s_len = k_ref.shape[1]
    s_bufs = (s0_ref, s1_ref)
    q_scale = head_dim ** -0.5 * LOG2_E
    lane = lax.broadcasted_iota(I32, (tq, LANES), 1)

    def lanes_of(block):
        return slice(block * LANES, (block + 1) * LANES)

    def q_of(u):
        r, g = units[u]
        rows = slice(r * tq, (r + 1) * tq)
        q = _rms(q_ref[0, rows, lanes_of(g)], head_dim) * gain_ref[...]
        q = _rope(q, c_ref[rows, :], sa_ref[rows, :], sb_ref[rows, :]) * q_scale
        return q.astype(BF16)

    def score_pass(u, j, q, m128):
        cols = slice(j * ATT_TK, (j + 1) * ATT_TK)
        keys = k_ref[0, cols, lanes_of(k_block[units[u][1]])]
        s = lax.dot_general(q, keys, _NT, preferred_element_type=F32)
        s_bufs[u % 2][:, cols] = s
        for c in range(ATT_TK // LANES):
            m128 = jnp.maximum(m128, s[:, c * LANES:(c + 1) * LANES])
        return m128

    def value_pass(u, j, m, acc):
        cols = slice(j * ATT_TK2, (j + 1) * ATT_TK2)
        p = jnp.exp2(s_bufs[u % 2][:, cols] - m)
        return acc + _dot(p.astype(BF16), v_ref[0, cols, lanes_of(v_block[units[u][1]])])

    neg = jnp.full((tq, LANES), -1e30, F32)
    m128 = neg
    q_first = q_of(0)
    for j in range(s_len // ATT_TK):
        m128 = score_pass(0, j, q_first, m128)
    ratio = ATT_TK2 // ATT_TK
    for u in range(len(units)):
        m = jnp.max(m128, axis=-1, keepdims=True)
        acc = jnp.zeros((tq, LANES), F32)
        has_next = u + 1 < len(units)
        if has_next:
            q_next = q_of(u + 1)
            m128 = neg
        for j2 in range(s_len // ATT_TK2):
            acc = value_pass(u, j2, m, acc)
            if has_next:
                for j in range(j2 * ratio, (j2 + 1) * ratio):
                    m128 = score_pass(u + 1, j, q_next, m128)
        r, g = units[u]
        ones = _ones_lane(g % 2)
        out = acc / acc[:, ones:ones + 1]
        if g % 2 == 0:
            out_even = out
        else:
            pair = jnp.where(lane < HALF, out_even, out)
            o_ref[0, r * tq:(r + 1) * tq, lanes_of(g // 2)] = pair.astype(BF16)


def _flash_call(q, gain, tabs, k, v, heads, k_block, v_block, head_dim, name):
    b, s, qw = q.shape
    n_steps = qw // (heads * LANES)
    kw = k.shape[2] // n_steps
    vw = v.shape[2] // n_steps
    assert heads % 2 == 0 and ATT_UNITS % heads == 0 and len(k_block) == len(v_block) == heads
    assert s % ATT_TK2 == 0 and ATT_TK2 % ATT_TK == 0
    row_tiles = ATT_UNITS // heads
    rows = ATT_TQ * row_tiles
    assert s % rows == 0
    units = tuple((r, g) for r in range(row_tiles) for g in range(heads))
    q_spec = pl.BlockSpec((1, rows, heads * LANES), lambda bi, kh, qi: (bi, qi, kh))
    o_spec = pl.BlockSpec((1, rows, heads * HALF), lambda bi, kh, qi: (bi, qi, kh))
    k_spec = pl.BlockSpec((1, s, kw), lambda bi, kh, qi: (bi, 0, kh))
    v_spec = pl.BlockSpec((1, s, vw), lambda bi, kh, qi: (bi, 0, kh))
    tab_spec = pl.BlockSpec((rows, LANES), lambda bi, kh, qi: (qi, 0))
    gain_spec = pl.BlockSpec((1, LANES), lambda bi, kh, qi: (0, 0))
    return pl.pallas_call(
        functools.partial(_flash_kernel, units=units, k_block=k_block, v_block=v_block, head_dim=head_dim),
        grid=(b, n_steps, s // rows),
        in_specs=[q_spec, gain_spec, tab_spec, tab_spec, tab_spec, k_spec, v_spec],
        out_specs=o_spec,
        out_shape=jax.ShapeDtypeStruct((b, s, qw // 2), BF16),
        scratch_shapes=[pltpu.VMEM((ATT_TQ, s), F32), pltpu.VMEM((ATT_TQ, s), F32)],
        compiler_params=pltpu.CompilerParams(
            dimension_semantics=("arbitrary", "arbitrary", "arbitrary"),
            vmem_limit_bytes=VMEM_LIMIT),
        name=name,
    )(q, gain, *tabs, k, v)


def _post_kernel(x_ref, ya_ref, yb_ref, ga_ref, gb_ref, wa_ref, wb_ref, wout_ref,
                 gffn_ref, wrh_ref, wrl_ref, br_ref, x1_ref, h2_ref, aff_ref, aff_rows_ref):
    a = _dot(ya_ref[...], wa_ref[...])
    b = _dot(yb_ref[...], wb_ref[...])
    merged = ga_ref[...].astype(F32) * a + gb_ref[...].astype(F32) * b
    x1 = x_ref[...] + _dot(merged.astype(BF16), wout_ref[...])
    x1_ref[...] = x1
    h2 = _rms(x1, D_MODEL) * gffn_ref[...]
    h2_hi = h2.astype(BF16)
    h2_ref[...] = h2_hi
    h2_lo = (h2 - h2_hi.astype(F32)).astype(BF16)
    wrh = wrh_ref[...]
    logits = _dot(h2_hi, wrh) + _dot(h2_lo, wrh) + _dot(h2_hi, wrl_ref[...]) + br_ref[...]
    e = jnp.exp(logits - jnp.max(logits, axis=-1, keepdims=True))
    aff = e / jnp.sum(e, axis=-1, keepdims=True)
    aff_ref[...] = aff
    aff_rows_ref[...] = jnp.transpose(aff)[:N_EXPERTS]


def _post_call(x2d, ya, yb, ga, gb, wa, wb, wout, gffn, wrh, wrl, br):
    t = x2d.shape[0]
    tm = POST_TM

    def full(a):
        return pl.BlockSpec(a.shape, lambda i: (0,) * a.ndim)

    def tok(width):
        return pl.BlockSpec((tm, width), lambda i: (i, 0))

    params = (wa, wb, wout, gffn, wrh, wrl, br)
    return pl.pallas_call(
        _post_kernel,
        grid=(t // tm,),
        in_specs=[tok(a.shape[1]) for a in (x2d, ya, yb, ga, gb)] + [full(a) for a in params],
        out_specs=[tok(D_MODEL), tok(D_MODEL), tok(LANES),
                   pl.BlockSpec((N_EXPERTS, tm), lambda i: (0, i))],
        out_shape=[jax.ShapeDtypeStruct((t, D_MODEL), F32),
                   jax.ShapeDtypeStruct((t, D_MODEL), BF16),
                   jax.ShapeDtypeStruct((t, LANES), F32),
                   jax.ShapeDtypeStruct((N_EXPERTS, t), F32)],
        compiler_params=pltpu.CompilerParams(
            dimension_semantics=("arbitrary",), vmem_limit_bytes=VMEM_LIMIT),
        name="post_attention",
    )(x2d, ya, yb, ga, gb, *params)


def _window_meta(start, end, cap):
    lane = lax.broadcasted_iota(I32, start.shape, 1)
    first = jnp.minimum(jnp.floor(start * (1.0 / SLOT_ALIGN)) * SLOT_ALIGN, float(cap - MOE_WIN))
    over = jnp.where((lane < N_EXPERTS) & (end - first > MOE_WIN), 1.0, 0.0)
    flag = jnp.max(over, axis=-1, keepdims=True)
    return jnp.where(lane == META_FLAG_LANE, flag, first).astype(I32)


def _route_kernel(aff_ref, aff_rows_ref, slot_t_ref, gate_t_ref, slot_r_ref, meta_ref, *, cap):
    s = aff_ref.shape[1]

    def count(mask):
        return jnp.sum(jnp.where(mask, 1.0, 0.0), axis=1, keepdims=True)

    def as_float(bits):
        return lax.bitcast_convert_type(bits, F32)

    def search(i, bits):
        cand = bits | jnp.left_shift(jnp.int32(1), 29 - i)
        return jnp.where(count(aff_rows_ref[...] >= as_float(cand)) >= cap, cand, bits)

    bits = lax.fori_loop(0, 30, search, jnp.zeros((N_EXPERTS, 1), I32))
    is_normal = bits >= MIN_NORMAL_BITS
    thr_col = jnp.where(is_normal, as_float(bits), 0.0)
    next_col = as_float(jnp.where(is_normal, bits + 1, MIN_NORMAL_BITS))
    need_col = cap - count(aff_rows_ref[...] >= next_col)

    def to_lanes(col):
        on_diag = (lax.broadcasted_iota(I32, (N_EXPERTS, LANES), 0)
                   == lax.broadcasted_iota(I32, (N_EXPERTS, LANES), 1))
        return jnp.sum(jnp.where(on_diag, col, 0.0), axis=0, keepdims=True)

    thr, thr_next, need = to_lanes(thr_col), to_lanes(next_col), to_lanes(need_col)

    blk = ROUTE_BLK
    tri = (lax.broadcasted_iota(I32, (blk, blk), 0) >= lax.broadcasted_iota(I32, (blk, blk), 1)).astype(BF16)
    eye = (lax.broadcasted_iota(I32, (LANES, LANES), 0) == lax.broadcasted_iota(I32, (LANES, LANES), 1)).astype(BF16)
    carry_eq = jnp.zeros((1, LANES), F32)
    carry_sel = jnp.zeros((1, LANES), F32)
    for c in range(s // blk):
        rows = slice(c * blk, (c + 1) * blk)
        ab = aff_ref[0, rows, :]
        gt = ab >= thr_next
        eq = jnp.where((ab >= thr) & jnp.logical_not(gt), 1.0, 0.0)
        eq_incl = _dot(tri, eq.astype(BF16)) + carry_eq
        carry_eq = eq_incl[blk - 1:blk, :]
        sel = jnp.where(gt | ((eq > 0.0) & (eq_incl - eq < need)), 1.0, 0.0)
        sel_incl = _dot(tri, sel.astype(BF16)) + carry_sel
        edge = carry_sel
        for k in range(blk // MOE_TC):
            nxt = sel_incl[(k + 1) * MOE_TC - 1:(k + 1) * MOE_TC, :]
            chunk = c * (blk // MOE_TC) + k
            meta_ref[0, chunk:chunk + 1, :] = _window_meta(edge, nxt, cap)
            edge = nxt
        carry_sel = sel_incl[blk - 1:blk, :]
        slot1 = sel * sel_incl
        slot_t_ref[0, rows, :] = slot1.astype(I32) - 1
        gate_t_ref[0, rows, :] = sel * aff_ref[0, rows, :]
        hi = jnp.floor(slot1 * (1.0 / 32.0))
        lo = slot1 - 32.0 * hi
        rows_hi = lax.dot_general(eye, hi.astype(BF16), _NT, preferred_element_type=F32)
        rows_lo = lax.dot_general(eye, lo.astype(BF16), _NT, preferred_element_type=F32)
        slot_rows = (32.0 * rows_hi + rows_lo).astype(I32) - 1
        slot_r_ref[0, :, rows] = slot_rows[:N_EXPERTS]


def _route_call(aff, aff_rows, cap):
    b, s, _ = aff.shape
    tok_spec = pl.BlockSpec((1, s, LANES), lambda bi: (bi, 0, 0))
    return pl.pallas_call(
        functools.partial(_route_kernel, cap=cap),
        grid=(b,),
        in_specs=[tok_spec, pl.BlockSpec((N_EXPERTS, s), lambda bi: (0, bi))],
        out_specs=[tok_spec, tok_spec, pl.BlockSpec((1, N_EXPERTS, s), lambda bi: (bi, 0, 0)),
                   pl.BlockSpec((1, s // MOE_TC, LANES), lambda bi: (bi, 0, 0))],
        out_shape=[jax.ShapeDtypeStruct((b, s, LANES), I32),
                   jax.ShapeDtypeStruct((b, s, LANES), F32),
                   jax.ShapeDtypeStruct((b, N_EXPERTS, s), I32),
                   jax.ShapeDtypeStruct((b, s // MOE_TC, LANES), I32)],
        compiler_params=pltpu.CompilerParams(
            dimension_semantics=("arbitrary",), vmem_limit_bytes=VMEM_LIMIT),
        name="route",
    )(aff, aff_rows)


def _meta_spec():
    return pl.BlockSpec((1, MOE_STEP_CHUNKS, 1, LANES), lambda bi, ci: (bi, ci, 0, 0),
                        memory_space=pltpu.SMEM)


def _dispatch_kernel(meta_ref, slot_r_ref, h2_ref, xe_ref, *, cap):
    tc = MOE_TC

    @pl.when(pl.program_id(1) == 0)
    def _():
        xe_ref[...] = jnp.zeros_like(xe_ref)

    def add_rows(first_row, rows):
        n = rows.shape[0]
        cur = xe_ref[0, pl.ds(first_row, n), :].astype(F32)
        xe_ref[0, pl.ds(first_row, n), :] = (cur + rows).astype(BF16)

    for k in range(MOE_STEP_CHUNKS):
        toks = slice(k * tc, (k + 1) * tc)
        slot_rows = slot_r_ref[0, :, toks]
        h2c = h2_ref[0, toks, :]
        fits = meta_ref[0, k, 0, META_FLAG_LANE] == 0

        @pl.when(fits)
        def _(k=k, slot_rows=slot_rows, h2c=h2c):
            firsts = [pl.multiple_of(meta_ref[0, k, 0, e], SLOT_ALIGN) for e in range(N_EXPERTS)]
            win_ids = lax.broadcasted_iota(I32, (MOE_WIN, tc), 0)
            half = N_EXPERTS // 2
            for lo in (0, half):
                pick = jnp.concatenate(
                    [jnp.where(win_ids == slot_rows[e:e + 1, :] - firsts[e], 1.0, 0.0).astype(BF16)
                     for e in range(lo, lo + half)], axis=0)
                got = _dot(pick, h2c)
                for i, e in enumerate(range(lo, lo + half)):
                    add_rows(e * cap + firsts[e], got[i * MOE_WIN:(i + 1) * MOE_WIN])

        @pl.when(jnp.logical_not(fits))
        def _(slot_rows=slot_rows, h2c=h2c):
            slot_ids = lax.broadcasted_iota(I32, (cap, tc), 0)
            for e in range(N_EXPERTS):
                pick = jnp.where(slot_ids == slot_rows[e:e + 1, :], 1.0, 0.0).astype(BF16)
                add_rows(e * cap, _dot(pick, h2c))


def _dispatch_call(meta, slot_r, h2, cap):
    b, s, d = h2.shape
    tc = MOE_TC * MOE_STEP_CHUNKS
    return pl.pallas_call(
        functools.partial(_dispatch_kernel, cap=cap),
        grid=(b, s // tc),
        in_specs=[_meta_spec(),
                  pl.BlockSpec((1, N_EXPERTS, tc), lambda bi, ci: (bi, 0, ci)),
                  pl.BlockSpec((1, tc, d), lambda bi, ci: (bi, ci, 0))],
        out_specs=pl.BlockSpec((1, N_EXPERTS * cap, d), lambda bi, ci: (bi, 0, 0)),
        out_shape=jax.ShapeDtypeStruct((b, N_EXPERTS * cap, d), BF16),
        compiler_params=pltpu.CompilerParams(
            dimension_semantics=("arbitrary", "arbitrary"), vmem_limit_bytes=VMEM_LIMIT),
        name="dispatch",
    )(meta, slot_r, h2)


def _expert_kernel(xe_ref, wg_ref, wu_ref, wd_ref, y_ref, wg_bf, wu_bf, wd_bf):
    @pl.when(pl.program_id(1) == 0)
    def _():
        wg_bf[...] = wg_ref[0].astype(BF16)
        wu_bf[...] = wu_ref[0].astype(BF16)
        wd_bf[...] = wd_ref[0].astype(BF16)

    nseq, cap, d = xe_ref.shape
    xe = xe_ref[...].reshape(nseq * cap, d)
    a = _dot(xe, wg_bf[...])
    u = _dot(xe, wu_bf[...])
    act = (a * jax.nn.sigmoid(a) * u).astype(BF16)
    y_ref[...] = _dot(act, wd_bf[...]).astype(BF16).reshape(nseq, cap, d)


def _expert_call(xe, wg, wu, wd, cap):
    b, _, d = xe.shape
    nseq = EXPERT_SEQS if b % EXPERT_SEQS == 0 else 1
    up_spec = pl.BlockSpec((1, d, EXPERT_FF), lambda e, bi: (e, 0, 0))
    down_spec = pl.BlockSpec((1, EXPERT_FF, d), lambda e, bi: (e, 0, 0))
    tok_spec = pl.BlockSpec((nseq, cap, d), lambda e, bi: (bi, e, 0))
    return pl.pallas_call(
        _expert_kernel,
        grid=(N_EXPERTS, b // nseq),
        in_specs=[tok_spec, up_spec, up_spec, down_spec],
        out_specs=tok_spec,
        out_shape=jax.ShapeDtypeStruct(xe.shape, BF16),
        scratch_shapes=[pltpu.VMEM((d, EXPERT_FF), BF16), pltpu.VMEM((d, EXPERT_FF), BF16),
                        pltpu.VMEM((EXPERT_FF, d), BF16)],
        compiler_params=pltpu.CompilerParams(
            dimension_semantics=("arbitrary", "arbitrary"), vmem_limit_bytes=VMEM_LIMIT),
        name="experts",
    )(xe, wg, wu, wd)


def _combine_kernel(meta_ref, x1_ref, slot_t_ref, gate_t_ref, y_ref, o_ref, ywin_ref, *, cap):
    tc = MOE_TC
    cols = N_EXPERTS * MOE_WIN
    win_shift = MOE_WIN.bit_length() - 1

    def fast(k, toks):
        col_expert = jnp.right_shift(lax.broadcasted_iota(I32, (LANES, cols), 1), win_shift)
        spread = jnp.where(lax.broadcasted_iota(I32, (LANES, cols), 0) == col_expert, 1.0, 0.0).astype(BF16)
        slot1 = (slot_t_ref[0, toks, :] + 1).astype(F32)
        hi = jnp.floor(slot1 * (1.0 / 32.0))
        lo = slot1 - 32.0 * hi
        slot1_cols = 32.0 * _dot(hi.astype(BF16), spread) + _dot(lo.astype(BF16), spread)
        gate_cols = _dot(gate_t_ref[0, toks, :].astype(BF16), spread)
        col = lax.broadcasted_iota(I32, (1, cols), 1)
        first1_cols = jnp.zeros((1, cols), F32)
        for e in range(N_EXPERTS):
            first = pl.multiple_of(meta_ref[0, k, 0, e], SLOT_ALIGN)
            ywin_ref[e * MOE_WIN:(e + 1) * MOE_WIN, :] = y_ref[0, pl.ds(e * cap + first, MOE_WIN), :]
            first1_cols = jnp.where(jnp.right_shift(col, win_shift) == e, (first + 1).astype(F32), first1_cols)
        hit = slot1_cols - first1_cols == jnp.bitwise_and(col, MOE_WIN - 1).astype(F32)
        place = jnp.where(hit, gate_cols, 0.0).astype(BF16)
        o_ref[0, toks, :] = x1_ref[0, toks, :] + _dot(place, ywin_ref[...])

    def full_scan(toks):
        acc = x1_ref[0, toks, :]
        slot_t = slot_t_ref[0, toks, :]
        gate_t = gate_t_ref[0, toks, :]
        slot_ids = lax.broadcasted_iota(I32, (tc, cap), 1)
        for e in range(N_EXPERTS):
            place = jnp.where(slot_ids == slot_t[:, e:e + 1], gate_t[:, e:e + 1], 0.0).astype(BF16)
            acc = acc + _dot(place, y_ref[0, e * cap:(e + 1) * cap, :])
        o_ref[0, toks, :] = acc

    for k in range(MOE_STEP_CHUNKS):
        toks = slice(k * tc, (k + 1) * tc)
        fits = meta_ref[0, k, 0, META_FLAG_LANE] == 0
        pl.when(fits)(functools.partial(fast, k, toks))
        pl.when(jnp.logical_not(fits))(functools.partial(full_scan, toks))


def _combine_call(meta, x1, slot_t, gate_t, y, cap):
    b, s, d = x1.shape
    tc = MOE_TC * MOE_STEP_CHUNKS
    return pl.pallas_call(
        functools.partial(_combine_kernel, cap=cap),
        grid=(b, s // tc),
        in_specs=[_meta_spec(),
                  pl.BlockSpec((1, tc, d), lambda bi, ci: (bi, ci, 0)),
                  pl.BlockSpec((1, tc, LANES), lambda bi, ci: (bi, ci, 0)),
                  pl.BlockSpec((1, tc, LANES), lambda bi, ci: (bi, ci, 0)),
                  pl.BlockSpec((1, N_EXPERTS * cap, d), lambda bi, ci: (bi, 0, 0))],
        out_specs=pl.BlockSpec((1, tc, d), lambda bi, ci: (bi, ci, 0)),
        out_shape=jax.ShapeDtypeStruct((b, s, d), F32),
        scratch_shapes=[pltpu.VMEM((N_EXPERTS * MOE_WIN, d), BF16)],
        compiler_params=pltpu.CompilerParams(
            dimension_semantics=("arbitrary", "arbitrary"), vmem_limit_bytes=VMEM_LIMIT),
        name="combine",
    )(meta, x1, slot_t, gate_t, y)


def _rope_tables(n, rot_dim, lane_lo, lane_hi, period):
    rows = n // GRID_W
    row = jnp.broadcast_to(jnp.arange(rows)[:, None], (rows, GRID_W)).reshape(n).astype(F32)
    col = jnp.broadcast_to(jnp.arange(GRID_W)[None, :], (rows, GRID_W)).reshape(n).astype(F32)
    nf = rot_dim // 4
    inv = ROPE_THETA ** (-jnp.arange(nf, dtype=F32) / nf)
    ang = jnp.concatenate([row[:, None] * inv, col[:, None] * inv], axis=-1)
    cos, sin = jnp.cos(ang), jnp.sin(ang)
    lane = np.arange(LANES)
    active = (lane >= lane_lo) & (lane < lane_hi)
    pair = np.where(active, ((lane - lane_lo) % period) // 2, 0)
    even = (lane % 2) == 0
    c = jnp.where(active[None, :], cos[:, pair], 1.0)
    sa = jnp.where((active & even)[None, :], -sin[:, pair], 0.0)
    sb = jnp.where((active & ~even)[None, :], sin[:, pair], 0.0)
    return c, sa, sb


def _pad_cols(w, heads, width):
    k = w.shape[0]
    w = w.reshape(k, heads, width)
    return jnp.pad(w, ((0, 0), (0, 0), (0, LANES - width))).reshape(k, heads * LANES)


def kernel(x, g_attn_norm, w_in, b_gate, g_q_lat, w_q_up, g_kv_lat, w_kv_up, g_mla_qnorm, g_mla_knorm, g_gqa_qnorm, g_gqa_knorm, w_mla_branch, w_gqa_branch, w_out, g_ffn_norm, w_router, b_router, w_exp_gate, w_exp_up, w_exp_down):
    b, s, d = x.shape
    assert d == D_MODEL and s % max(PRE_TM, ROUTE_BLK) == 0 and ROUTE_BLK % MOE_TC == 0
    cap = CAPACITY_FACTOR * s // N_EXPERTS
    assert cap >= MOE_WIN and cap % SLOT_ALIGN == 0
    t = b * s

    splits = np.cumsum([Q_LORA, KV_LORA, MLA_ROPE, GQA_HEADS * GQA_HEAD_DIM,
                        GQA_KV_HEADS * GQA_HEAD_DIM, GQA_KV_HEADS * GQA_HEAD_DIM, D_MODEL])
    w_ql, w_kvl, w_kr, w_qg, w_kg, w_vg, w_ga, w_gb = jnp.split(w_in, splits, axis=1)
    w_kr = jnp.pad(w_kr, ((0, 0), (MLA_NOPE, LANES - MLA_QK)))
    w_qg = _pad_cols(w_qg, GQA_HEADS, GQA_HEAD_DIM)
    w_kg = _pad_cols(w_kg, GQA_KV_HEADS, GQA_HEAD_DIM)
    w_vg = w_vg.reshape(d, GQA_KV_HEADS, 1, GQA_HEAD_DIM)
    zeros = jnp.zeros_like(w_vg)
    w_vg = jnp.concatenate([jnp.concatenate([w_vg, zeros], axis=3),
                            jnp.concatenate([zeros, w_vg], axis=3)], axis=2).reshape(d, -1)
    win = jnp.concatenate([w_ql, w_kvl, w_kr, w_qg, w_kg, w_vg, w_ga, w_gb], axis=1).astype(BF16)
    assert win.shape[1] == _C_END
    wqup = _pad_cols(w_q_up, MLA_HEADS, MLA_QK).astype(BF16)
    v_even = w_kv_up.reshape(KV_LORA, MLA_HEADS, LANES)[:, 0::2, MLA_NOPE:]
    wkvup = jnp.concatenate([w_kv_up, _pad_cols(v_even.reshape(KV_LORA, -1), MLA_HEADS // 2, MLA_V)],
                            axis=1).astype(BF16)

    def row(v):
        return v.reshape(1, -1).astype(F32)

    gq = row(jnp.pad(g_mla_qnorm, (0, LANES - MLA_QK)))
    gk = row(jnp.pad(g_mla_knorm, (0, LANES - MLA_QK)))
    ggq = row(jnp.pad(g_gqa_qnorm, (0, LANES - GQA_HEAD_DIM)))
    ggk = row(jnp.pad(g_gqa_knorm, (0, LANES - GQA_HEAD_DIM)))

    assert MLA_V == HALF and GQA_HEAD_DIM == HALF
    wa = w_mla_branch.astype(BF16)
    wb = w_gqa_branch.astype(BF16)

    wr = jnp.pad(w_router, ((0, 0), (0, LANES - N_EXPERTS)))
    wr_hi = wr.astype(BF16)
    wr_lo = (wr - wr_hi.astype(F32)).astype(BF16)
    br = jnp.concatenate([b_router.astype(F32), jnp.full((LANES - N_EXPERTS,), -1e30, F32)]).reshape(1, LANES)

    tabs_a = _rope_tables(s, MLA_ROPE, MLA_NOPE, MLA_QK, MLA_ROPE)
    tabs_b = _rope_tables(s, GQA_HEAD_DIM, 0, GQA_HEAD_DIM, GQA_HEAD_DIM)

    x2d = x.reshape(t, d)
    qa, ka, va, qb, kb, vb, ga, gb = _pre_call(
        x2d, s, row(g_attn_norm), win, row(g_q_lat), wqup, row(g_kv_lat), wkvup,
        gk, ggk, row(b_gate), tabs_a + tabs_b)

    def seq(a):
        return a.reshape(b, s, a.shape[-1])

    ya = _flash_call(seq(qa), gq, tabs_a, seq(ka), seq(va), 2, (0, 1), (0, 1), MLA_QK, "attention_mla")
    yb = _flash_call(seq(qb), ggq, tabs_b, seq(kb), seq(vb), GQA_GROUP, (0,) * GQA_GROUP,
                     tuple(g % 2 for g in range(GQA_GROUP)), GQA_HEAD_DIM, "attention_gqa")

    x1, h2, aff, aff_rows = _post_call(x2d, ya.reshape(t, -1), yb.reshape(t, -1), ga, gb,
                                       wa, wb, w_out.astype(BF16), row(g_ffn_norm), wr_hi, wr_lo, br)

    slot_t, gate_t, slot_r, meta = _route_call(seq(aff), aff_rows, cap)
    meta = meta.reshape(b, s // MOE_TC, 1, LANES)
    xe = _dispatch_call(meta, slot_r, seq(h2), cap)
    y = _expert_call(xe, w_exp_gate, w_exp_up, w_exp_down, cap)
    return _combine_call(meta, seq(x1), slot_t, gate_t, y, cap)
```

```python
import functools

import jax
import jax.numpy as jnp
import numpy as np
from jax import lax
from jax.experimental import pallas as pl
from jax.experimental.pallas import tpu as pltpu

F32 = jnp.float32
BF16 = jnp.bfloat16
I32 = jnp.int32

D_MODEL = 1024
GRID_W = 64
ROPE_THETA = 10000.0
EPS = 1e-6

MLA_HEADS = 8
MLA_NOPE = 64
MLA_ROPE = 32
MLA_QK = MLA_NOPE + MLA_ROPE
MLA_V = 64
Q_LORA = 768
KV_LORA = 256

GQA_HEADS = 8
GQA_KV_HEADS = 2
GQA_HEAD_DIM = 64
GQA_GROUP = GQA_HEADS // GQA_KV_HEADS

N_EXPERTS = 16
CAPACITY_FACTOR = 2
EXPERT_FF = 1024

MIN_NORMAL_BITS = 0x00800000
LANES = 128
HEADS_WIDTH = MLA_HEADS * LANES

_C_QLAT = 0
_C_KVLAT = _C_QLAT + Q_LORA
_C_KROPE = _C_KVLAT + KV_LORA
_C_QG = _C_KROPE + LANES
_C_KG = _C_QG + HEADS_WIDTH
_C_VG = _C_KG + GQA_KV_HEADS * LANES
_C_GA = _C_VG + 2 * GQA_KV_HEADS * LANES
_C_GB = _C_GA + D_MODEL
_C_END = _C_GB + D_MODEL

HALF = LANES // 2
LOG2_E = 1.4426950408889634

PRE_TM = 256
POST_TM = 512
EXPERT_SEQS = 2
ATT_TQ = 256
ATT_UNITS = 16
ATT_TK = 512
ATT_TK2 = 1024
ROUTE_BLK = 512
MOE_TC = 256
MOE_STEP_CHUNKS = 4
MOE_WIN = 64
SLOT_ALIGN = 16
META_FLAG_LANE = N_EXPERTS
VMEM_LIMIT = 56 * 1024 * 1024

_NT = (((1,), (1,)), ((), ()))


def _dot(a, b):
    return jnp.dot(a, b, preferred_element_type=F32)


def _rms(x, width):
    return x * lax.rsqrt(jnp.sum(x * x, axis=-1, keepdims=True) * (1.0 / width) + EPS)


def _value_block(x, lane, half):
    keep = (lane >= half * HALF) & (lane < (half + 1) * HALF)
    return jnp.where(lane == _ones_lane(half), 1.0, jnp.where(keep, x, 0.0))


def _ones_lane(out_half):
    return (1 - out_half) * HALF


def _rope(x, c, sa, sb):
    return x * c + pltpu.roll(x, LANES - 1, 1) * sa + pltpu.roll(x, 1, 1) * sb


def _pre_kernel(x_ref, gattn_ref, win_ref, gql_ref, wqup_ref, gkvl_ref, wkvup_ref,
                gk_ref, ggk_ref, bgate_ref,
                cm_ref, sam_ref, sbm_ref, cg_ref, sag_ref, sbg_ref,
                qa_ref, ka_ref, va_ref, qb_ref, kb_ref, vb_ref, ga_ref, gb_ref):
    x = x_ref[...]
    h = (_rms(x, D_MODEL) * gattn_ref[...]).astype(BF16)

    def proj(lo, hi):
        return _dot(h, win_ref[:, lo:hi])

    lane = lax.broadcasted_iota(I32, (x.shape[0], LANES), 1)
    low_half = lane < GQA_HEAD_DIM
    cm, sam, sbm = cm_ref[...], sam_ref[...], sbm_ref[...]
    cg, sag, sbg = cg_ref[...], sag_ref[...], sbg_ref[...]

    latents = proj(_C_QLAT, _C_QG)
    c_q = (_rms(latents[:, _C_QLAT:_C_KVLAT], Q_LORA) * gql_ref[...]).astype(BF16)
    qa_ref[...] = _dot(c_q, wqup_ref[...])

    c_kv = (_rms(latents[:, _C_KVLAT:_C_KROPE], KV_LORA) * gkvl_ref[...]).astype(BF16)
    k_rope = latents[:, _C_KROPE:_C_QG]
    kv_up = _dot(c_kv, wkvup_ref[...])
    gk = gk_ref[...]
    for hd in range(MLA_HEADS):
        kv = kv_up[:, hd * LANES:(hd + 1) * LANES]
        kseg = jnp.where(low_half, kv, 0.0) + k_rope
        kseg = _rms(kseg, MLA_QK) * gk
        ka_ref[:, hd * LANES:(hd + 1) * LANES] = _rope(kseg, cm, sam, sbm).astype(BF16)
        if hd % 2 == 0:
            kv = kv_up[:, (MLA_HEADS + hd // 2) * LANES:(MLA_HEADS + hd // 2 + 1) * LANES]
        va_ref[:, hd * LANES:(hd + 1) * LANES] = _value_block(kv, lane, hd % 2).astype(BF16)

    qb_ref[...] = proj(_C_QG, _C_KG)
    ggk = ggk_ref[...]
    kvg = proj(_C_KG, _C_GA)
    kv_w = GQA_KV_HEADS * LANES
    for hd in range(GQA_KV_HEADS):
        seg = _rms(kvg[:, hd * LANES:(hd + 1) * LANES], GQA_HEAD_DIM) * ggk
        kb_ref[:, hd * LANES:(hd + 1) * LANES] = _rope(seg, cg, sag, sbg).astype(BF16)
        for half in range(2):
            blk = 2 * hd + half
            vraw = kvg[:, kv_w + blk * LANES:kv_w + (blk + 1) * LANES]
            vb_ref[:, blk * LANES:(blk + 1) * LANES] = _value_block(vraw, lane, half).astype(BF16)

    gates = jax.nn.sigmoid(proj(_C_GA, _C_END) + bgate_ref[...])
    ga_ref[...] = gates[:, :D_MODEL].astype(BF16)
    gb_ref[...] = gates[:, D_MODEL:].astype(BF16)


def _pre_call(x2d, seq, gattn, win, gql, wqup, gkvl, wkvup, gk, ggk, bgate, tabs):
    t = x2d.shape[0]
    tm = PRE_TM
    n_seq_tiles = seq // tm

    def full(a):
        return pl.BlockSpec(a.shape, lambda i: (0,) * a.ndim)

    def tok(width):
        return pl.BlockSpec((tm, width), lambda i: (i, 0))

    tab_spec = pl.BlockSpec((tm, LANES), lambda i: (i % n_seq_tiles, 0))
    params = (gattn, win, gql, wqup, gkvl, wkvup, gk, ggk, bgate)
    kv_w = GQA_KV_HEADS * LANES
    outs = ((HEADS_WIDTH, F32), (HEADS_WIDTH, BF16), (HEADS_WIDTH, BF16), (HEADS_WIDTH, F32),
            (kv_w, BF16), (2 * kv_w, BF16), (D_MODEL, BF16), (D_MODEL, BF16))
    return pl.pallas_call(
        _pre_kernel,
        grid=(t // tm,),
        in_specs=[tok(D_MODEL)] + [full(a) for a in params] + [tab_spec] * 6,
        out_specs=[tok(w) for w, _ in outs],
        out_shape=[jax.ShapeDtypeStruct((t, w), dt) for w, dt in outs],
        compiler_params=pltpu.CompilerParams(
            dimension_semantics=("arbitrary",), vmem_limit_bytes=VMEM_LIMIT),
        name="pre_attention",
    )(x2d, *params, *tabs)


def _flash_kernel(q_ref, gain_ref, c_ref, sa_ref, sb_ref, k_ref, v_ref, o_ref, s0_ref, s1_ref, *,
                  units, k_block, v_block, head_dim):
    tq = s0_ref.shape[0]
    s_len = k_ref.shape[1]
    s_bufs = (s0_ref, s1_ref)
    q_scale = head_dim ** -0.5 * LOG2_E
    lane = lax.broadcasted_iota(I32, (tq, LANES), 1)

    def lanes_of(block):
        return slice(block * LANES, (block + 1) * LANES)

    def q_of(u):
        r, g = units[u]
        rows = slice(r * tq, (r + 1) * tq)
        q = _rms(q_ref[0, rows, lanes_of(g)], head_dim) * gain_ref[...]
        q = _rope(q, c_ref[rows, :], sa_ref[rows, :], sb_ref[rows, :]) * q_scale
        return q.astype(BF16)

    def score_pass(u, j, q, m128):
        cols = slice(j * ATT_TK, (j + 1) * ATT_TK)
        keys = k_ref[0, cols, lanes_of(k_block[units[u][1]])]
        s = lax.dot_general(q, keys, _NT, preferred_element_type=F32)
        s_bufs[u % 2][:, cols] = s
        for c in range(ATT_TK // LANES):
            m128 = jnp.maximum(m128, s[:, c * LANES:(c + 1) * LANES])
        return m128

    def value_pass(u, j, m, acc):
        cols = slice(j * ATT_TK2, (j + 1) * ATT_TK2)
        p = jnp.exp2(s_bufs[u % 2][:, cols] - m)
        return acc + _dot(p.astype(BF16), v_ref[0, cols, lanes_of(v_block[units[u][1]])])

    neg = jnp.full((tq, LANES), -1e30, F32)
    m128 = neg
    q_first = q_of(0)
    for j in range(s_len // ATT_TK):
        m128 = score_pass(0, j, q_first, m128)
    ratio = ATT_TK2 // ATT_TK
    for u in range(len(units)):
        m = jnp.max(m128, axis=-1, keepdims=True)
        acc = jnp.zeros((tq, LANES), F32)
        has_next = u + 1 < len(units)
        if has_next:
            q_next = q_of(u + 1)
            m128 = neg
        for j2 in range(s_len // ATT_TK2):
            acc = value_pass(u, j2, m, acc)
            if has_next:
                for j in range(j2 * ratio, (j2 + 1) * ratio):
                    m128 = score_pass(u + 1, j, q_next, m128)
        r, g = units[u]
        ones = _ones_lane(g % 2)
        out = acc / acc[:, ones:ones + 1]
        if g % 2 == 0:
            out_even = out
        else:
            pair = jnp.where(lane < HALF, out_even, out)
            o_ref[0, r * tq:(r + 1) * tq, lanes_of(g // 2)] = pair.astype(BF16)


def _flash_call(q, gain, tabs, k, v, heads, k_block, v_block, head_dim, name):
    b, s, qw = q.shape
    n_steps = qw // (heads * LANES)
    kw = k.shape[2] // n_steps
    vw = v.shape[2] // n_steps
    assert heads % 2 == 0 and ATT_UNITS % heads == 0 and len(k_block) == len(v_block) == heads
    assert s % ATT_TK2 == 0 and ATT_TK2 % ATT_TK == 0
    row_tiles = ATT_UNITS // heads
    rows = ATT_TQ * row_tiles
    assert s % rows == 0
    units = tuple((r, g) for r in range(row_tiles) for g in range(heads))
    q_spec = pl.BlockSpec((1, rows, heads * LANES), lambda bi, kh, qi: (bi, qi, kh))
    o_spec = pl.BlockSpec((1, rows, heads * HALF), lambda bi, kh, qi: (bi, qi, kh))
    k_spec = pl.BlockSpec((1, s, kw), lambda bi, kh, qi: (bi, 0, kh))
    v_spec = pl.BlockSpec((1, s, vw), lambda bi, kh, qi: (bi, 0, kh))
    tab_spec = pl.BlockSpec((rows, LANES), lambda bi, kh, qi: (qi, 0))
    gain_spec = pl.BlockSpec((1, LANES), lambda bi, kh, qi: (0, 0))
    return pl.pallas_call(
        functools.partial(_flash_kernel, units=units, k_block=k_block, v_block=v_block, head_dim=head_dim),
        grid=(b, n_steps, s // rows),
        in_specs=[q_spec, gain_spec, tab_spec, tab_spec, tab_spec, k_spec, v_spec],
        out_specs=o_spec,
        out_shape=jax.ShapeDtypeStruct((b, s, qw // 2), BF16),
        scratch_shapes=[pltpu.VMEM((ATT_TQ, s), F32), pltpu.VMEM((ATT_TQ, s), F32)],
        compiler_params=pltpu.CompilerParams(
            dimension_semantics=("arbitrary", "arbitrary", "arbitrary"),
            vmem_limit_bytes=VMEM_LIMIT),
        name=name,
    )(q, gain, *tabs, k, v)


def _post_kernel(x_ref, ya_ref, yb_ref, ga_ref, gb_ref, wa_ref, wb_ref, wout_ref,
                 gffn_ref, wrh_ref, wrl_ref, br_ref, x1_ref, h2_ref, aff_ref, aff_rows_ref):
    a = _dot(ya_ref[...], wa_ref[...])
    b = _dot(yb_ref[...], wb_ref[...])
    merged = ga_ref[...].astype(F32) * a + gb_ref[...].astype(F32) * b
    x1 = x_ref[...] + _dot(merged.astype(BF16), wout_ref[...])
    x1_ref[...] = x1
    h2 = _rms(x1, D_MODEL) * gffn_ref[...]
    h2_hi = h2.astype(BF16)
    h2_ref[...] = h2_hi
    h2_lo = (h2 - h2_hi.astype(F32)).astype(BF16)
    wrh = wrh_ref[...]
    logits = _dot(h2_hi, wrh) + _dot(h2_lo, wrh) + _dot(h2_hi, wrl_ref[...]) + br_ref[...]
    e = jnp.exp(logits - jnp.max(logits, axis=-1, keepdims=True))
    aff = e / jnp.sum(e, axis=-1, keepdims=True)
    aff_ref[...] = aff
    aff_rows_ref[...] = jnp.transpose(aff)[:N_EXPERTS]


def _post_call(x2d, ya, yb, ga, gb, wa, wb, wout, gffn, wrh, wrl, br):
    t = x2d.shape[0]
    tm = POST_TM

    def full(a):
        return pl.BlockSpec(a.shape, lambda i: (0,) * a.ndim)

    def tok(width):
        return pl.BlockSpec((tm, width), lambda i: (i, 0))

    params = (wa, wb, wout, gffn, wrh, wrl, br)
    return pl.pallas_call(
        _post_kernel,
        grid=(t // tm,),
        in_specs=[tok(a.shape[1]) for a in (x2d, ya, yb, ga, gb)] + [full(a) for a in params],
        out_specs=[tok(D_MODEL), tok(D_MODEL), tok(LANES),
                   pl.BlockSpec((N_EXPERTS, tm), lambda i: (0, i))],
        out_shape=[jax.ShapeDtypeStruct((t, D_MODEL), F32),
                   jax.ShapeDtypeStruct((t, D_MODEL), BF16),
                   jax.ShapeDtypeStruct((t, LANES), F32),
                   jax.ShapeDtypeStruct((N_EXPERTS, t), F32)],
        compiler_params=pltpu.CompilerParams(
            dimension_semantics=("arbitrary",), vmem_limit_bytes=VMEM_LIMIT),
        name="post_attention",
    )(x2d, ya, yb, ga, gb, *params)


def _window_meta(start, end, cap):
    lane = lax.broadcasted_iota(I32, start.shape, 1)
    first = jnp.minimum(jnp.floor(start * (1.0 / SLOT_ALIGN)) * SLOT_ALIGN, float(cap - MOE_WIN))
    over = jnp.where((lane < N_EXPERTS) & (end - first > MOE_WIN), 1.0, 0.0)
    flag = jnp.max(over, axis=-1, keepdims=True)
    return jnp.where(lane == META_FLAG_LANE, flag, first).astype(I32)


def _route_kernel(aff_ref, aff_rows_ref, slot_t_ref, gate_t_ref, slot_r_ref, meta_ref, *, cap):
    s = aff_ref.shape[1]

    def count(mask):
        return jnp.sum(jnp.where(mask, 1.0, 0.0), axis=1, keepdims=True)

    def as_float(bits):
        return lax.bitcast_convert_type(bits, F32)

    def search(i, bits):
        cand = bits | jnp.left_shift(jnp.int32(1), 29 - i)
        return jnp.where(count(aff_rows_ref[...] >= as_float(cand)) >= cap, cand, bits)

    bits = lax.fori_loop(0, 30, search, jnp.zeros((N_EXPERTS, 1), I32))
    is_normal = bits >= MIN_NORMAL_BITS
    thr_col = jnp.where(is_normal, as_float(bits), 0.0)
    next_col = as_float(jnp.where(is_normal, bits + 1, MIN_NORMAL_BITS))
    need_col = cap - count(aff_rows_ref[...] >= next_col)

    def to_lanes(col):
        on_diag = (lax.broadcasted_iota(I32, (N_EXPERTS, LANES), 0)
                   == lax.broadcasted_iota(I32, (N_EXPERTS, LANES), 1))
        return jnp.sum(jnp.where(on_diag, col, 0.0), axis=0, keepdims=True)

    thr, thr_next, need = to_lanes(thr_col), to_lanes(next_col), to_lanes(need_col)

    blk = ROUTE_BLK
    tri = (lax.broadcasted_iota(I32, (blk, blk), 0) >= lax.broadcasted_iota(I32, (blk, blk), 1)).astype(BF16)
    eye = (lax.broadcasted_iota(I32, (LANES, LANES), 0) == lax.broadcasted_iota(I32, (LANES, LANES), 1)).astype(BF16)
    carry_eq = jnp.zeros((1, LANES), F32)
    carry_sel = jnp.zeros((1, LANES), F32)
    for c in range(s // blk):
        rows = slice(c * blk, (c + 1) * blk)
        ab = aff_ref[0, rows, :]
        gt = ab >= thr_next
        eq = jnp.where((ab >= thr) & jnp.logical_not(gt), 1.0, 0.0)
        eq_incl = _dot(tri, eq.astype(BF16)) + carry_eq
        carry_eq = eq_incl[blk - 1:blk, :]
        sel = jnp.where(gt | ((eq > 0.0) & (eq_incl - eq < need)), 1.0, 0.0)
        sel_incl = _dot(tri, sel.astype(BF16)) + carry_sel
        edge = carry_sel
        for k in range(blk // MOE_TC):
            nxt = sel_incl[(k + 1) * MOE_TC - 1:(k + 1) * MOE_TC, :]
            chunk = c * (blk // MOE_TC) + k
            meta_ref[0, chunk:chunk + 1, :] = _window_meta(edge, nxt, cap)
            edge = nxt
        carry_sel = sel_incl[blk - 1:blk, :]
        slot1 = sel * sel_incl
        slot_t_ref[0, rows, :] = slot1.astype(I32) - 1
        gate_t_ref[0, rows, :] = sel * aff_ref[0, rows, :]
        hi = jnp.floor(slot1 * (1.0 / 32.0))
        lo = slot1 - 32.0 * hi
        rows_hi = lax.dot_general(eye, hi.astype(BF16), _NT, preferred_element_type=F32)
        rows_lo = lax.dot_general(eye, lo.astype(BF16), _NT, preferred_element_type=F32)
        slot_rows = (32.0 * rows_hi + rows_lo).astype(I32) - 1
        slot_r_ref[0, :, rows] = slot_rows[:N_EXPERTS]


def _route_call(aff, aff_rows, cap):
    b, s, _ = aff.shape
    tok_spec = pl.BlockSpec((1, s, LANES), lambda bi: (bi, 0, 0))
    return pl.pallas_call(
        functools.partial(_route_kernel, cap=cap),
        grid=(b,),
        in_specs=[tok_spec, pl.BlockSpec((N_EXPERTS, s), lambda bi: (0, bi))],
        out_specs=[tok_spec, tok_spec, pl.BlockSpec((1, N_EXPERTS, s), lambda bi: (bi, 0, 0)),
                   pl.BlockSpec((1, s // MOE_TC, LANES), lambda bi: (bi, 0, 0))],
        out_shape=[jax.ShapeDtypeStruct((b, s, LANES), I32),
                   jax.ShapeDtypeStruct((b, s, LANES), F32),
                   jax.ShapeDtypeStruct((b, N_EXPERTS, s), I32),
                   jax.ShapeDtypeStruct((b, s // MOE_TC, LANES), I32)],
        compiler_params=pltpu.CompilerParams(
            dimension_semantics=("arbitrary",), vmem_limit_bytes=VMEM_LIMIT),
        name="route",
    )(aff, aff_rows)


def _meta_spec():
    return pl.BlockSpec((1, MOE_STEP_CHUNKS, 1, LANES), lambda bi, ci: (bi, ci, 0, 0),
                        memory_space=pltpu.SMEM)


def _dispatch_kernel(meta_ref, slot_r_ref, h2_ref, xe_ref, *, cap):
    tc = MOE_TC

    @pl.when(pl.program_id(1) == 0)
    def _():
        xe_ref[...] = jnp.zeros_like(xe_ref)

    def add_rows(first_row, rows):
        n = rows.shape[0]
        cur = xe_ref[0, pl.ds(first_row, n), :].astype(F32)
        xe_ref[0, pl.ds(first_row, n), :] = (cur + rows).astype(BF16)

    for k in range(MOE_STEP_CHUNKS):
        toks = slice(k * tc, (k + 1) * tc)
        slot_rows = slot_r_ref[0, :, toks]
        h2c = h2_ref[0, toks, :]
        fits = meta_ref[0, k, 0, META_FLAG_LANE] == 0

        @pl.when(fits)
        def _(k=k, slot_rows=slot_rows, h2c=h2c):
            firsts = [pl.multiple_of(meta_ref[0, k, 0, e], SLOT_ALIGN) for e in range(N_EXPERTS)]
            win_ids = lax.broadcasted_iota(I32, (MOE_WIN, tc), 0)
            half = N_EXPERTS // 2
            for lo in (0, half):
                pick = jnp.concatenate(
                    [jnp.where(win_ids == slot_rows[e:e + 1, :] - firsts[e], 1.0, 0.0).astype(BF16)
                     for e in range(lo, lo + half)], axis=0)
                got = _dot(pick, h2c)
                for i, e in enumerate(range(lo, lo + half)):
                    add_rows(e * cap + firsts[e], got[i * MOE_WIN:(i + 1) * MOE_WIN])

        @pl.when(jnp.logical_not(fits))
        def _(slot_rows=slot_rows, h2c=h2c):
            slot_ids = lax.broadcasted_iota(I32, (cap, tc), 0)
            for e in range(N_EXPERTS):
                pick = jnp.where(slot_ids == slot_rows[e:e + 1, :], 1.0, 0.0).astype(BF16)
                add_rows(e * cap, _dot(pick, h2c))


def _dispatch_call(meta, slot_r, h2, cap):
    b, s, d = h2.shape
    tc = MOE_TC * MOE_STEP_CHUNKS
    return pl.pallas_call(
        functools.partial(_dispatch_kernel, cap=cap),
        grid=(b, s // tc),
        in_specs=[_meta_spec(),
                  pl.BlockSpec((1, N_EXPERTS, tc), lambda bi, ci: (bi, 0, ci)),
                  pl.BlockSpec((1, tc, d), lambda bi, ci: (bi, ci, 0))],
        out_specs=pl.BlockSpec((1, N_EXPERTS * cap, d), lambda bi, ci: (bi, 0, 0)),
        out_shape=jax.ShapeDtypeStruct((b, N_EXPERTS * cap, d), BF16),
        compiler_params=pltpu.CompilerParams(
            dimension_semantics=("arbitrary", "arbitrary"), vmem_limit_bytes=VMEM_LIMIT),
        name="dispatch",
    )(meta, slot_r, h2)


def _expert_kernel(xe_ref, wg_ref, wu_ref, wd_ref, y_ref, wg_bf, wu_bf, wd_bf):
    @pl.when(pl.program_id(1) == 0)
    def _():
        wg_bf[...] = wg_ref[0].astype(BF16)
        wu_bf[...] = wu_ref[0].astype(BF16)
        wd_bf[...] = wd_ref[0].astype(BF16)

    nseq, cap, d = xe_ref.shape
    xe = xe_ref[...].reshape(nseq * cap, d)
    a = _dot(xe, wg_bf[...])
    u = _dot(xe, wu_bf[...])
    act = (a * jax.nn.sigmoid(a) * u).astype(BF16)
    y_ref[...] = _dot(act, wd_bf[...]).astype(BF16).reshape(nseq, cap, d)


def _expert_call(xe, wg, wu, wd, cap):
    b, _, d = xe.shape
    nseq = EXPERT_SEQS if b % EXPERT_SEQS == 0 else 1
    up_spec = pl.BlockSpec((1, d, EXPERT_FF), lambda e, bi: (e, 0, 0))
    down_spec = pl.BlockSpec((1, EXPERT_FF, d), lambda e, bi: (e, 0, 0))
    tok_spec = pl.BlockSpec((nseq, cap, d), lambda e, bi: (bi, e, 0))
    return pl.pallas_call(
        _expert_kernel,
        grid=(N_EXPERTS, b // nseq),
        in_specs=[tok_spec, up_spec, up_spec, down_spec],
        out_specs=tok_spec,
        out_shape=jax.ShapeDtypeStruct(xe.shape, BF16),
        scratch_shapes=[pltpu.VMEM((d, EXPERT_FF), BF16), pltpu.VMEM((d, EXPERT_FF), BF16),
                        pltpu.VMEM((EXPERT_FF, d), BF16)],
        compiler_params=pltpu.CompilerParams(
            dimension_semantics=("arbitrary", "arbitrary"), vmem_limit_bytes=VMEM_LIMIT),
        name="experts",
    )(xe, wg, wu, wd)


def _combine_kernel(meta_ref, x1_ref, slot_t_ref, gate_t_ref, y_ref, o_ref, ywin_ref, *, cap):
    tc = MOE_TC
    cols = N_EXPERTS * MOE_WIN
    win_shift = MOE_WIN.bit_length() - 1

    def fast(k, toks):
        col_expert = jnp.right_shift(lax.broadcasted_iota(I32, (LANES, cols), 1), win_shift)
        spread = jnp.where(lax.broadcasted_iota(I32, (LANES, cols), 0) == col_expert, 1.0, 0.0).astype(BF16)
        slot1 = (slot_t_ref[0, toks, :] + 1).astype(F32)
        hi = jnp.floor(slot1 * (1.0 / 32.0))
        lo = slot1 - 32.0 * hi
        slot1_cols = 32.0 * _dot(hi.astype(BF16), spread) + _dot(lo.astype(BF16), spread)
        gate_cols = _dot(gate_t_ref[0, toks, :].astype(BF16), spread)
        col = lax.broadcasted_iota(I32, (1, cols), 1)
        first1_cols = jnp.zeros((1, cols), F32)
        for e in range(N_EXPERTS):
            first = pl.multiple_of(meta_ref[0, k, 0, e], SLOT_ALIGN)
            ywin_ref[e * MOE_WIN:(e + 1) * MOE_WIN, :] = y_ref[0, pl.ds(e * cap + first, MOE_WIN), :]
            first1_cols = jnp.where(jnp.right_shift(col, win_shift) == e, (first + 1).astype(F32), first1_cols)
        hit = slot1_cols - first1_cols == jnp.bitwise_and(col, MOE_WIN - 1).astype(F32)
        place = jnp.where(hit, gate_cols, 0.0).astype(BF16)
        o_ref[0, toks, :] = x1_ref[0, toks, :] + _dot(place, ywin_ref[...])

    def full_scan(toks):
        acc = x1_ref[0, toks, :]
        slot_t = slot_t_ref[0, toks, :]
        gate_t = gate_t_ref[0, toks, :]
        slot_ids = lax.broadcasted_iota(I32, (tc, cap), 1)
        for e in range(N_EXPERTS):
            place = jnp.where(slot_ids == slot_t[:, e:e + 1], gate_t[:, e:e + 1], 0.0).astype(BF16)
            acc = acc + _dot(place, y_ref[0, e * cap:(e + 1) * cap, :])
        o_ref[0, toks, :] = acc

    for k in range(MOE_STEP_CHUNKS):
        toks = slice(k * tc, (k + 1) * tc)
        fits = meta_ref[0, k, 0, META_FLAG_LANE] == 0
        pl.when(fits)(functools.partial(fast, k, toks))
        pl.when(jnp.logical_not(fits))(functools.partial(full_scan, toks))


def _combine_call(meta, x1, slot_t, gate_t, y, cap):
    b, s, d = x1.shape
    tc = MOE_TC * MOE_STEP_CHUNKS
    return pl.pallas_call(
        functools.partial(_combine_kernel, cap=cap),
        grid=(b, s // tc),
        in_specs=[_meta_spec(),
                  pl.BlockSpec((1, tc, d), lambda bi, ci: (bi, ci, 0)),
                  pl.BlockSpec((1, tc, LANES), lambda bi, ci: (bi, ci, 0)),
                  pl.BlockSpec((1, tc, LANES), lambda bi, ci: (bi, ci, 0)),
                  pl.BlockSpec((1, N_EXPERTS * cap, d), lambda bi, ci: (bi, 0, 0))],
        out_specs=pl.BlockSpec((1, tc, d), lambda bi, ci: (bi, ci, 0)),
        out_shape=jax.ShapeDtypeStruct((b, s, d), F32),
        scratch_shapes=[pltpu.VMEM((N_EXPERTS * MOE_WIN, d), BF16)],
        compiler_params=pltpu.CompilerParams(
            dimension_semantics=("arbitrary", "arbitrary"), vmem_limit_bytes=VMEM_LIMIT),
        name="combine",
    )(meta, x1, slot_t, gate_t, y)


def _rope_tables(n, rot_dim, lane_lo, lane_hi, period):
    rows = n // GRID_W
    row = jnp.broadcast_to(jnp.arange(rows)[:, None], (rows, GRID_W)).reshape(n).astype(F32)
    col = jnp.broadcast_to(jnp.arange(GRID_W)[None, :], (rows, GRID_W)).reshape(n).astype(F32)
    nf = rot_dim // 4
    inv = ROPE_THETA ** (-jnp.arange(nf, dtype=F32) / nf)
    ang = jnp.concatenate([row[:, None] * inv, col[:, None] * inv], axis=-1)
    cos, sin = jnp.cos(ang), jnp.sin(ang)
    lane = np.arange(LANES)
    active = (lane >= lane_lo) & (lane < lane_hi)
    pair = np.where(active, ((lane - lane_lo) % period) // 2, 0)
    even = (lane % 2) == 0
    c = jnp.where(active[None, :], cos[:, pair], 1.0)
    sa = jnp.where((active & even)[None, :], -sin[:, pair], 0.0)
    sb = jnp.where((active & ~even)[None, :], sin[:, pair], 0.0)
    return c, sa, sb


def _pad_cols(w, heads, width):
    k = w.shape[0]
    w = w.reshape(k, heads, width)
    return jnp.pad(w, ((0, 0), (0, 0), (0, LANES - width))).reshape(k, heads * LANES)


def kernel(x, g_attn_norm, w_in, b_gate, g_q_lat, w_q_up, g_kv_lat, w_kv_up, g_mla_qnorm, g_mla_knorm, g_gqa_qnorm, g_gqa_knorm, w_mla_branch, w_gqa_branch, w_out, g_ffn_norm, w_router, b_router, w_exp_gate, w_exp_up, w_exp_down):
    b, s, d = x.shape
    assert d == D_MODEL and s % max(PRE_TM, ROUTE_BLK) == 0 and ROUTE_BLK % MOE_TC == 0
    assert s % (MOE_TC * MOE_STEP_CHUNKS) == 0 and (b * s) % POST_TM == 0
    cap = CAPACITY_FACTOR * s // N_EXPERTS
    assert cap >= MOE_WIN and cap % SLOT_ALIGN == 0
    t = b * s

    splits = np.cumsum([Q_LORA, KV_LORA, MLA_ROPE, GQA_HEADS * GQA_HEAD_DIM,
                        GQA_KV_HEADS * GQA_HEAD_DIM, GQA_KV_HEADS * GQA_HEAD_DIM, D_MODEL])
    w_ql, w_kvl, w_kr, w_qg, w_kg, w_vg, w_ga, w_gb = jnp.split(w_in, splits, axis=1)
    w_kr = jnp.pad(w_kr, ((0, 0), (MLA_NOPE, LANES - MLA_QK)))
    w_qg = _pad_cols(w_qg, GQA_HEADS, GQA_HEAD_DIM)
    w_kg = _pad_cols(w_kg, GQA_KV_HEADS, GQA_HEAD_DIM)
    w_vg = w_vg.reshape(d, GQA_KV_HEADS, 1, GQA_HEAD_DIM)
    zeros = jnp.zeros_like(w_vg)
    w_vg = jnp.concatenate([jnp.concatenate([w_vg, zeros], axis=3),
                            jnp.concatenate([zeros, w_vg], axis=3)], axis=2).reshape(d, -1)
    win = jnp.concatenate([w_ql, w_kvl, w_kr, w_qg, w_kg, w_vg, w_ga, w_gb], axis=1).astype(BF16)
    assert win.shape[1] == _C_END
    wqup = _pad_cols(w_q_up, MLA_HEADS, MLA_QK).astype(BF16)
    v_even = w_kv_up.reshape(KV_LORA, MLA_HEADS, LANES)[:, 0::2, MLA_NOPE:]
    wkvup = jnp.concatenate([w_kv_up, _pad_cols(v_even.reshape(KV_LORA, -1), MLA_HEADS // 2, MLA_V)],
                            axis=1).astype(BF16)

    def row(v):
        return v.reshape(1, -1).astype(F32)

    gq = row(jnp.pad(g_mla_qnorm, (0, LANES - MLA_QK)))
    gk = row(jnp.pad(g_mla_knorm, (0, LANES - MLA_QK)))
    ggq = row(jnp.pad(g_gqa_qnorm, (0, LANES - GQA_HEAD_DIM)))
    ggk = row(jnp.pad(g_gqa_knorm, (0, LANES - GQA_HEAD_DIM)))

    assert MLA_V == HALF and GQA_HEAD_DIM == HALF
    wa = w_mla_branch.astype(BF16)
    wb = w_gqa_branch.astype(BF16)

    wr = jnp.pad(w_router, ((0, 0), (0, LANES - N_EXPERTS)))
    wr_hi = wr.astype(BF16)
    wr_lo = (wr - wr_hi.astype(F32)).astype(BF16)
    br = jnp.concatenate([b_router.astype(F32), jnp.full((LANES - N_EXPERTS,), -1e30, F32)]).reshape(1, LANES)

    tabs_a = _rope_tables(s, MLA_ROPE, MLA_NOPE, MLA_QK, MLA_ROPE)
    tabs_b = _rope_tables(s, GQA_HEAD_DIM, 0, GQA_HEAD_DIM, GQA_HEAD_DIM)

    x2d = x.reshape(t, d)
    qa, ka, va, qb, kb, vb, ga, gb = _pre_call(
        x2d, s, row(g_attn_norm), win, row(g_q_lat), wqup, row(g_kv_lat), wkvup,
        gk, ggk, row(b_gate), tabs_a + tabs_b)

    def seq(a):
        return a.reshape(b, s, a.shape[-1])

    ya = _flash_call(seq(qa), gq, tabs_a, seq(ka), seq(va), 2, (0, 1), (0, 1), MLA_QK, "attention_mla")
    yb = _flash_call(seq(qb), ggq, tabs_b, seq(kb), seq(vb), GQA_GROUP, (0,) * GQA_GROUP,
                     tuple(g % 2 for g in range(GQA_GROUP)), GQA_HEAD_DIM, "attention_gqa")

    x1, h2, aff, aff_rows = _post_call(x2d, ya.reshape(t, -1), yb.reshape(t, -1), ga, gb,
                                       wa, wb, w_out.astype(BF16), row(g_ffn_norm), wr_hi, wr_lo, br)

    slot_t, gate_t, slot_r, meta = _route_call(seq(aff), aff_rows, cap)
    meta = meta.reshape(b, s // MOE_TC, 1, LANES)
    xe = _dispatch_call(meta, slot_r, seq(h2), cap)
    y = _expert_call(xe, w_exp_gate, w_exp_up, w_exp_down, cap)
    return _combine_call(meta, seq(x1), slot_t, gate_t, y, cap)
```

```python
import functools

import jax
import jax.numpy as jnp
import numpy as np
from jax import lax
from jax.experimental import pallas as pl
from jax.experimental.pallas import tpu as pltpu

F32 = jnp.float32
BF16 = jnp.bfloat16
I32 = jnp.int32

D_MODEL = 1024
GRID_W = 64
ROPE_THETA = 10000.0
EPS = 1e-6

MLA_HEADS = 8
MLA_NOPE = 64
MLA_ROPE = 32
MLA_QK = MLA_NOPE + MLA_ROPE
MLA_V = 64
Q_LORA = 768
KV_LORA = 256

GQA_HEADS = 8
GQA_KV_HEADS = 2
GQA_HEAD_DIM = 64
GQA_GROUP = GQA_HEADS // GQA_KV_HEADS

N_EXPERTS = 16
CAPACITY_FACTOR = 2
EXPERT_FF = 1024

MIN_NORMAL_BITS = 0x00800000
LANES = 128
HEADS_WIDTH = MLA_HEADS * LANES

_C_QLAT = 0
_C_KVLAT = _C_QLAT + Q_LORA
_C_KROPE = _C_KVLAT + KV_LORA
_C_QG = _C_KROPE + LANES
_C_KG = _C_QG + GQA_HEADS * GQA_HEAD_DIM
_C_VG = _C_KG + GQA_KV_HEADS * LANES
_C_GA = _C_VG + 2 * GQA_KV_HEADS * LANES
_C_GB = _C_GA + D_MODEL
_C_END = _C_GB + D_MODEL

HALF = LANES // 2
LOG2_E = 1.4426950408889634

PRE_TM = 256
POST_TM = 512
EXPERT_SEQS = 2
ATT_TQ = 256
ATT_UNITS = 16
ATT_TK = 512
ATT_TK2 = 1024
ROUTE_BLK = 512
MOE_TC = 256
MOE_STEP_CHUNKS = 4
MOE_WIN = 64
SLOT_ALIGN = 16
META_FLAG_LANE = N_EXPERTS
VMEM_LIMIT = 56 * 1024 * 1024

_NT = (((1,), (1,)), ((), ()))


def _dot(a, b):
    return jnp.dot(a, b, preferred_element_type=F32)


def _rms(x, width):
    return x * lax.rsqrt(jnp.sum(x * x, axis=-1, keepdims=True) * (1.0 / width) + EPS)


def _value_block(x, lane, half):
    keep = (lane >= half * HALF) & (lane < (half + 1) * HALF)
    return jnp.where(lane == _ones_lane(half), 1.0, jnp.where(keep, x, 0.0))


def _ones_lane(out_half):
    return (1 - out_half) * HALF


def _rope(x, c, sa, sb):
    return x * c + pltpu.roll(x, LANES - 1, 1) * sa + pltpu.roll(x, 1, 1) * sb


def _pre_kernel(x_ref, gattn_ref, win_ref, gql_ref, wqup_ref, gkvl_ref, wkvup_ref,
                gk_ref, ggk_ref, bgate_ref,
                cm_ref, sam_ref, sbm_ref, cg_ref, sag_ref, sbg_ref,
                qa_ref, ka_ref, va_ref, qb_ref, kb_ref, vb_ref, ga_ref, gb_ref):
    x = x_ref[...]
    h = (_rms(x, D_MODEL) * gattn_ref[...]).astype(BF16)

    def proj(lo, hi):
        return _dot(h, win_ref[:, lo:hi])

    lane = lax.broadcasted_iota(I32, (x.shape[0], LANES), 1)
    low_half = lane < GQA_HEAD_DIM
    cm, sam, sbm = cm_ref[...], sam_ref[...], sbm_ref[...]
    cg, sag, sbg = cg_ref[...], sag_ref[...], sbg_ref[...]

    latents = proj(_C_QLAT, _C_QG)
    c_q = (_rms(latents[:, _C_QLAT:_C_KVLAT], Q_LORA) * gql_ref[...]).astype(BF16)
    qa_ref[...] = _dot(c_q, wqup_ref[...])

    c_kv = (_rms(latents[:, _C_KVLAT:_C_KROPE], KV_LORA) * gkvl_ref[...]).astype(BF16)
    k_rope = latents[:, _C_KROPE:_C_QG]
    kv_up = _dot(c_kv, wkvup_ref[...])
    gk = gk_ref[...]
    for hd in range(MLA_HEADS):
        kv = kv_up[:, hd * LANES:(hd + 1) * LANES]
        kseg = jnp.where(low_half, kv, 0.0) + k_rope
        kseg = _rms(kseg, MLA_QK) * gk
        ka_ref[:, hd * LANES:(hd + 1) * LANES] = _rope(kseg, cm, sam, sbm).astype(BF16)
        if hd % 2 == 0:
            kv = kv_up[:, (MLA_HEADS + hd // 2) * LANES:(MLA_HEADS + hd // 2 + 1) * LANES]
        va_ref[:, hd * LANES:(hd + 1) * LANES] = _value_block(kv, lane, hd % 2).astype(BF16)

    qb_ref[...] = proj(_C_QG, _C_KG)
    ggk = ggk_ref[...]
    kvg = proj(_C_KG, _C_GA)
    kv_w = GQA_KV_HEADS * LANES
    for hd in range(GQA_KV_HEADS):
        seg = _rms(kvg[:, hd * LANES:(hd + 1) * LANES], LANES) * ggk
        kb_ref[:, hd * LANES:(hd + 1) * LANES] = _rope(seg, cg, sag, sbg).astype(BF16)
        for half in range(2):
            blk = 2 * hd + half
            vraw = kvg[:, kv_w + blk * LANES:kv_w + (blk + 1) * LANES]
            vb_ref[:, blk * LANES:(blk + 1) * LANES] = _value_block(vraw, lane, half).astype(BF16)

    gates = jax.nn.sigmoid(proj(_C_GA, _C_END) + bgate_ref[...])
    ga_ref[...] = gates[:, :D_MODEL].astype(BF16)
    gb_ref[...] = gates[:, D_MODEL:].astype(BF16)


def _pre_call(x2d, seq, gattn, win, gql, wqup, gkvl, wkvup, gk, ggk, bgate, tabs):
    t = x2d.shape[0]
    tm = PRE_TM
    n_seq_tiles = seq // tm

    def full(a):
        return pl.BlockSpec(a.shape, lambda i: (0,) * a.ndim)

    def tok(width):
        return pl.BlockSpec((tm, width), lambda i: (i, 0))

    tab_spec = pl.BlockSpec((tm, LANES), lambda i: (i % n_seq_tiles, 0))
    params = (gattn, win, gql, wqup, gkvl, wkvup, gk, ggk, bgate)
    kv_w = GQA_KV_HEADS * LANES
    outs = ((HEADS_WIDTH, F32), (HEADS_WIDTH, BF16), (HEADS_WIDTH, BF16), (GQA_HEADS * GQA_HEAD_DIM, F32),
            (kv_w, BF16), (2 * kv_w, BF16), (D_MODEL, BF16), (D_MODEL, BF16))
    return pl.pallas_call(
        _pre_kernel,
        grid=(t // tm,),
        in_specs=[tok(D_MODEL)] + [full(a) for a in params] + [tab_spec] * 6,
        out_specs=[tok(w) for w, _ in outs],
        out_shape=[jax.ShapeDtypeStruct((t, w), dt) for w, dt in outs],
        compiler_params=pltpu.CompilerParams(
            dimension_semantics=("arbitrary",), vmem_limit_bytes=VMEM_LIMIT),
        name="pre_attention",
    )(x2d, *params, *tabs)


def _flash_kernel(q_ref, gain_ref, c_ref, sa_ref, sb_ref, k_ref, v_ref, o_ref, s0_ref, s1_ref, *,
                  units, q_block, k_block, v_block, head_dim):
    tq = s0_ref.shape[0]
    s_len = k_ref.shape[1]
    s_bufs = (s0_ref, s1_ref)
    q_scale = head_dim ** -0.5 * LOG2_E
    lane = lax.broadcasted_iota(I32, (tq, LANES), 1)

    def lanes_of(block):
        return slice(block * LANES, (block + 1) * LANES)

    def q_of(u):
        r, g = units[u]
        rows = slice(r * tq, (r + 1) * tq)
        block, half = q_block[g]
        q = q_ref[0, rows, lanes_of(block)]
        if half is not None:
            q = jnp.where((lane >= half * HALF) & (lane < (half + 1) * HALF), q, 0.0)
        q = _rms(q, head_dim) * gain_ref[...]
        q = _rope(q, c_ref[rows, :], sa_ref[rows, :], sb_ref[rows, :]) * q_scale
        return q.astype(BF16)

    def score_pass(u, j, q, m128):
        cols = slice(j * ATT_TK, (j + 1) * ATT_TK)
        keys = k_ref[0, cols, lanes_of(k_block[units[u][1]])]
        s = lax.dot_general(q, keys, _NT, preferred_element_type=F32)
        s_bufs[u % 2][:, cols] = s
        for c in range(ATT_TK // LANES):
            m128 = jnp.maximum(m128, s[:, c * LANES:(c + 1) * LANES])
        return m128

    def value_pass(u, j, m, acc):
        cols = slice(j * ATT_TK2, (j + 1) * ATT_TK2)
        p = jnp.exp2(s_bufs[u % 2][:, cols] - m)
        return acc + _dot(p.astype(BF16), v_ref[0, cols, lanes_of(v_block[units[u][1]])])

    neg = jnp.full((tq, LANES), -1e30, F32)
    m128 = neg
    q_first = q_of(0)
    for j in range(s_len // ATT_TK):
        m128 = score_pass(0, j, q_first, m128)
    ratio = ATT_TK2 // ATT_TK
    for u in range(len(units)):
        m = jnp.max(m128, axis=-1, keepdims=True)
        acc = jnp.zeros((tq, LANES), F32)
        has_next = u + 1 < len(units)
        if has_next:
            q_next = q_of(u + 1)
            m128 = neg
        for j2 in range(s_len // ATT_TK2):
            acc = value_pass(u, j2, m, acc)
            if has_next:
                for j in range(j2 * ratio, (j2 + 1) * ratio):
                    m128 = score_pass(u + 1, j, q_next, m128)
        r, g = units[u]
        ones = _ones_lane(g % 2)
        out = acc / acc[:, ones:ones + 1]
        if g % 2 == 0:
            out_even = out
        else:
            pair = jnp.where(lane < HALF, out_even, out)
            o_ref[0, r * tq:(r + 1) * tq, lanes_of(g // 2)] = pair.astype(BF16)


def _flash_call(q, gain, tabs, k, v, n_heads, heads, q_block, k_block, v_block, head_dim, name):
    b, s, qw = q.shape
    n_steps = n_heads // heads
    qw_step = qw // n_steps
    kw = k.shape[2] // n_steps
    vw = v.shape[2] // n_steps
    assert heads % 2 == 0 and ATT_UNITS % heads == 0
    assert len(q_block) == len(k_block) == len(v_block) == heads
    assert s % ATT_TK2 == 0 and ATT_TK2 % ATT_TK == 0
    row_tiles = ATT_UNITS // heads
    rows = ATT_TQ * row_tiles
    assert s % rows == 0
    units = tuple((r, g) for r in range(row_tiles) for g in range(heads))
    q_spec = pl.BlockSpec((1, rows, qw_step), lambda bi, kh, qi: (bi, qi, kh))
    o_spec = pl.BlockSpec((1, rows, heads * HALF), lambda bi, kh, qi: (bi, qi, kh))
    k_spec = pl.BlockSpec((1, s, kw), lambda bi, kh, qi: (bi, 0, kh))
    v_spec = pl.BlockSpec((1, s, vw), lambda bi, kh, qi: (bi, 0, kh))
    tab_spec = pl.BlockSpec((rows, LANES), lambda bi, kh, qi: (qi, 0))
    gain_spec = pl.BlockSpec((1, LANES), lambda bi, kh, qi: (0, 0))
    return pl.pallas_call(
        functools.partial(_flash_kernel, units=units, q_block=q_block, k_block=k_block, v_block=v_block,
                          head_dim=head_dim),
        grid=(b, n_steps, s // rows),
        in_specs=[q_spec, gain_spec, tab_spec, tab_spec, tab_spec, k_spec, v_spec],
        out_specs=o_spec,
        out_shape=jax.ShapeDtypeStruct((b, s, n_heads * HALF), BF16),
        scratch_shapes=[pltpu.VMEM((ATT_TQ, s), F32), pltpu.VMEM((ATT_TQ, s), F32)],
        compiler_params=pltpu.CompilerParams(
            dimension_semantics=("arbitrary", "arbitrary", "arbitrary"),
            vmem_limit_bytes=VMEM_LIMIT),
        name=name,
    )(q, gain, *tabs, k, v)


def _post_kernel(x_ref, ya_ref, yb_ref, ga_ref, gb_ref, wa_ref, wb_ref, wout_ref,
                 gffn_ref, wrh_ref, wrl_ref, br_ref, x1_ref, h2_ref, aff_ref, aff_rows_ref):
    a = _dot(ya_ref[...], wa_ref[...])
    b = _dot(yb_ref[...], wb_ref[...])
    merged = ga_ref[...].astype(F32) * a + gb_ref[...].astype(F32) * b
    x1 = x_ref[...] + _dot(merged.astype(BF16), wout_ref[...])
    x1_ref[...] = x1
    h2 = _rms(x1, D_MODEL) * gffn_ref[...]
    h2_hi = h2.astype(BF16)
    h2_ref[...] = h2_hi
    h2_lo = (h2 - h2_hi.astype(F32)).astype(BF16)
    wrh = wrh_ref[...]
    logits = _dot(h2_hi, wrh) + _dot(h2_lo, wrh) + _dot(h2_hi, wrl_ref[...]) + br_ref[...]
    e = jnp.exp(logits - jnp.max(logits, axis=-1, keepdims=True))
    aff = e / jnp.sum(e, axis=-1, keepdims=True)
    aff_ref[...] = aff
    aff_rows_ref[...] = jnp.transpose(aff)[:N_EXPERTS]


def _post_call(x2d, ya, yb, ga, gb, wa, wb, wout, gffn, wrh, wrl, br):
    t = x2d.shape[0]
    tm = POST_TM

    def full(a):
        return pl.BlockSpec(a.shape, lambda i: (0,) * a.ndim)

    def tok(width):
        return pl.BlockSpec((tm, width), lambda i: (i, 0))

    params = (wa, wb, wout, gffn, wrh, wrl, br)
    return pl.pallas_call(
        _post_kernel,
        grid=(t // tm,),
        in_specs=[tok(a.shape[1]) for a in (x2d, ya, yb, ga, gb)] + [full(a) for a in params],
        out_specs=[tok(D_MODEL), tok(D_MODEL), tok(LANES),
                   pl.BlockSpec((N_EXPERTS, tm), lambda i: (0, i))],
        out_shape=[jax.ShapeDtypeStruct((t, D_MODEL), F32),
                   jax.ShapeDtypeStruct((t, D_MODEL), BF16),
                   jax.ShapeDtypeStruct((t, LANES), F32),
                   jax.ShapeDtypeStruct((N_EXPERTS, t), F32)],
        compiler_params=pltpu.CompilerParams(
            dimension_semantics=("arbitrary",), vmem_limit_bytes=VMEM_LIMIT),
        name="post_attention",
    )(x2d, ya, yb, ga, gb, *params)


def _window_meta(start, end, cap):
    lane = lax.broadcasted_iota(I32, start.shape, 1)
    first = jnp.minimum(jnp.floor(start * (1.0 / SLOT_ALIGN)) * SLOT_ALIGN, float(cap - MOE_WIN))
    over = jnp.where((lane < N_EXPERTS) & (end - first > MOE_WIN), 1.0, 0.0)
    flag = jnp.max(over, axis=-1, keepdims=True)
    return jnp.where(lane == META_FLAG_LANE, flag, first).astype(I32)


def _route_kernel(aff_ref, aff_rows_ref, slot_t_ref, gate_t_ref, slot_r_ref, meta_ref, *, cap):
    s = aff_ref.shape[1]

    def count(mask):
        return jnp.sum(jnp.where(mask, 1.0, 0.0), axis=1, keepdims=True)

    def as_float(bits):
        return lax.bitcast_convert_type(bits, F32)

    def search(i, bits):
        cand = bits | jnp.left_shift(jnp.int32(1), 29 - i)
        return jnp.where(count(aff_rows_ref[...] >= as_float(cand)) >= cap, cand, bits)

    bits = lax.fori_loop(0, 30, search, jnp.zeros((N_EXPERTS, 1), I32))
    is_normal = bits >= MIN_NORMAL_BITS
    thr_col = jnp.where(is_normal, as_float(bits), 0.0)
    next_col = as_float(jnp.where(is_normal, bits + 1, MIN_NORMAL_BITS))
    need_col = cap - count(aff_rows_ref[...] >= next_col)

    def to_lanes(col):
        on_diag = (lax.broadcasted_iota(I32, (N_EXPERTS, LANES), 0)
                   == lax.broadcasted_iota(I32, (N_EXPERTS, LANES), 1))
        return jnp.sum(jnp.where(on_diag, col, 0.0), axis=0, keepdims=True)

    thr, thr_next, need = to_lanes(thr_col), to_lanes(next_col), to_lanes(need_col)

    blk = ROUTE_BLK
    tri = (lax.broadcasted_iota(I32, (blk, blk), 0) >= lax.broadcasted_iota(I32, (blk, blk), 1)).astype(BF16)
    eye = (lax.broadcasted_iota(I32, (LANES, LANES), 0) == lax.broadcasted_iota(I32, (LANES, LANES), 1)).astype(BF16)
    carry_eq = jnp.zeros((1, LANES), F32)
    carry_sel = jnp.zeros((1, LANES), F32)
    for c in range(s // blk):
        rows = slice(c * blk, (c + 1) * blk)
        ab = aff_ref[0, rows, :]
        gt = ab >= thr_next
        eq = jnp.where((ab >= thr) & jnp.logical_not(gt), 1.0, 0.0)
        eq_incl = _dot(tri, eq.astype(BF16)) + carry_eq
        carry_eq = eq_incl[blk - 1:blk, :]
        sel = jnp.where(gt | ((eq > 0.0) & (eq_incl - eq < need)), 1.0, 0.0)
        sel_incl = _dot(tri, sel.astype(BF16)) + carry_sel
        edge = carry_sel
        for k in range(blk // MOE_TC):
            nxt = sel_incl[(k + 1) * MOE_TC - 1:(k + 1) * MOE_TC, :]
            chunk = c * (blk // MOE_TC) + k
            meta_ref[0, chunk:chunk + 1, :] = _window_meta(edge, nxt, cap)
            edge = nxt
        carry_sel = sel_incl[blk - 1:blk, :]
        slot1 = sel * sel_incl
        slot_t_ref[0, rows, :] = slot1.astype(I32) - 1
        gate_t_ref[0, rows, :] = sel * aff_ref[0, rows, :]
        hi = jnp.floor(slot1 * (1.0 / 32.0))
        lo = slot1 - 32.0 * hi
        rows_hi = lax.dot_general(eye, hi.astype(BF16), _NT, preferred_element_type=F32)
        rows_lo = lax.dot_general(eye, lo.astype(BF16), _NT, preferred_element_type=F32)
        slot_rows = (32.0 * rows_hi + rows_lo).astype(I32) - 1
        slot_r_ref[0, :, rows] = slot_rows[:N_EXPERTS]


def _route_call(aff, aff_rows, cap):
    b, s, _ = aff.shape
    tok_spec = pl.BlockSpec((1, s, LANES), lambda bi: (bi, 0, 0))
    return pl.pallas_call(
        functools.partial(_route_kernel, cap=cap),
        grid=(b,),
        in_specs=[tok_spec, pl.BlockSpec((N_EXPERTS, s), lambda bi: (0, bi))],
        out_specs=[tok_spec, tok_spec, pl.BlockSpec((1, N_EXPERTS, s), lambda bi: (bi, 0, 0)),
                   pl.BlockSpec((1, s // MOE_TC, LANES), lambda bi: (bi, 0, 0))],
        out_shape=[jax.ShapeDtypeStruct((b, s, LANES), I32),
                   jax.ShapeDtypeStruct((b, s, LANES), F32),
                   jax.ShapeDtypeStruct((b, N_EXPERTS, s), I32),
                   jax.ShapeDtypeStruct((b, s // MOE_TC, LANES), I32)],
        compiler_params=pltpu.CompilerParams(
            dimension_semantics=("arbitrary",), vmem_limit_bytes=VMEM_LIMIT),
        name="route",
    )(aff, aff_rows)


def _meta_spec():
    return pl.BlockSpec((1, MOE_STEP_CHUNKS, 1, LANES), lambda bi, ci: (bi, ci, 0, 0),
                        memory_space=pltpu.SMEM)


def _dispatch_kernel(meta_ref, slot_r_ref, h2_ref, xe_ref, *, cap):
    tc = MOE_TC

    @pl.when(pl.program_id(1) == 0)
    def _():
        xe_ref[...] = jnp.zeros_like(xe_ref)

    def add_rows(first_row, rows):
        n = rows.shape[0]
        cur = xe_ref[0, pl.ds(first_row, n), :].astype(F32)
        xe_ref[0, pl.ds(first_row, n), :] = (cur + rows).astype(BF16)

    for k in range(MOE_STEP_CHUNKS):
        toks = slice(k * tc, (k + 1) * tc)
        slot_rows = slot_r_ref[0, :, toks]
        h2c = h2_ref[0, toks, :]
        fits = meta_ref[0, k, 0, META_FLAG_LANE] == 0

        @pl.when(fits)
        def _(k=k, slot_rows=slot_rows, h2c=h2c):
            firsts = [pl.multiple_of(meta_ref[0, k, 0, e], SLOT_ALIGN) for e in range(N_EXPERTS)]
            win_ids = lax.broadcasted_iota(I32, (MOE_WIN, tc), 0)
            half = N_EXPERTS // 2
            for lo in (0, half):
                pick = jnp.concatenate(
                    [jnp.where(win_ids == slot_rows[e:e + 1, :] - firsts[e], 1.0, 0.0).astype(BF16)
                     for e in range(lo, lo + half)], axis=0)
                got = _dot(pick, h2c)
                for i, e in enumerate(range(lo, lo + half)):
                    add_rows(e * cap + firsts[e], got[i * MOE_WIN:(i + 1) * MOE_WIN])

        @pl.when(jnp.logical_not(fits))
        def _(slot_rows=slot_rows, h2c=h2c):
            slot_ids = lax.broadcasted_iota(I32, (cap, tc), 0)
            for e in range(N_EXPERTS):
                pick = jnp.where(slot_ids == slot_rows[e:e + 1, :], 1.0, 0.0).astype(BF16)
                add_rows(e * cap, _dot(pick, h2c))


def _dispatch_call(meta, slot_r, h2, cap):
    b, s, d = h2.shape
    tc = MOE_TC * MOE_STEP_CHUNKS
    return pl.pallas_call(
        functools.partial(_dispatch_kernel, cap=cap),
        grid=(b, s // tc),
        in_specs=[_meta_spec(),
                  pl.BlockSpec((1, N_EXPERTS, tc), lambda bi, ci: (bi, 0, ci)),
                  pl.BlockSpec((1, tc, d), lambda bi, ci: (bi, ci, 0))],
        out_specs=pl.BlockSpec((1, N_EXPERTS * cap, d), lambda bi, ci: (bi, 0, 0)),
        out_shape=jax.ShapeDtypeStruct((b, N_EXPERTS * cap, d), BF16),
        compiler_params=pltpu.CompilerParams(
            dimension_semantics=("arbitrary", "arbitrary"), vmem_limit_bytes=VMEM_LIMIT),
        name="dispatch",
    )(meta, slot_r, h2)


def _expert_kernel(xe_ref, wg_ref, wu_ref, wd_ref, y_ref, wg_bf, wu_bf, wd_bf):
    @pl.when(pl.program_id(1) == 0)
    def _():
        wg_bf[...] = wg_ref[0].astype(BF16)
        wu_bf[...] = wu_ref[0].astype(BF16)
        wd_bf[...] = wd_ref[0].astype(BF16)

    nseq, cap, d = xe_ref.shape
    xe = xe_ref[...].reshape(nseq * cap, d)
    a = _dot(xe, wg_bf[...])
    u = _dot(xe, wu_bf[...])
    act = (a * jax.nn.sigmoid(a) * u).astype(BF16)
    y_ref[...] = _dot(act, wd_bf[...]).astype(BF16).reshape(nseq, cap, d)


def _expert_call(xe, wg, wu, wd, cap):
    b, _, d = xe.shape
    nseq = EXPERT_SEQS if b % EXPERT_SEQS == 0 else 1
    up_spec = pl.BlockSpec((1, d, EXPERT_FF), lambda e, bi: (e, 0, 0))
    down_spec = pl.BlockSpec((1, EXPERT_FF, d), lambda e, bi: (e, 0, 0))
    tok_spec = pl.BlockSpec((nseq, cap, d), lambda e, bi: (bi, e, 0))
    return pl.pallas_call(
        _expert_kernel,
        grid=(N_EXPERTS, b // nseq),
        in_specs=[tok_spec, up_spec, up_spec, down_spec],
        out_specs=tok_spec,
        out_shape=jax.ShapeDtypeStruct(xe.shape, BF16),
        scratch_shapes=[pltpu.VMEM((d, EXPERT_FF), BF16), pltpu.VMEM((d, EXPERT_FF), BF16),
                        pltpu.VMEM((EXPERT_FF, d), BF16)],
        compiler_params=pltpu.CompilerParams(
            dimension_semantics=("arbitrary", "arbitrary"), vmem_limit_bytes=VMEM_LIMIT),
        name="experts",
    )(xe, wg, wu, wd)


def _combine_kernel(meta_ref, x1_ref, slot_t_ref, gate_t_ref, y_ref, o_ref, ywin_ref, *, cap):
    tc = MOE_TC
    cols = N_EXPERTS * MOE_WIN
    win_shift = MOE_WIN.bit_length() - 1

    def fast(k, toks):
        col_expert = jnp.right_shift(lax.broadcasted_iota(I32, (LANES, cols), 1), win_shift)
        spread = jnp.where(lax.broadcasted_iota(I32, (LANES, cols), 0) == col_expert, 1.0, 0.0).astype(BF16)
        slot1 = (slot_t_ref[0, toks, :] + 1).astype(F32)
        hi = jnp.floor(slot1 * (1.0 / 32.0))
        lo = slot1 - 32.0 * hi
        slot1_cols = 32.0 * _dot(hi.astype(BF16), spread) + _dot(lo.astype(BF16), spread)
        gate_cols = _dot(gate_t_ref[0, toks, :].astype(BF16), spread)
        col = lax.broadcasted_iota(I32, (1, cols), 1)
        first1_cols = jnp.zeros((1, cols), F32)
        for e in range(N_EXPERTS):
            first = pl.multiple_of(meta_ref[0, k, 0, e], SLOT_ALIGN)
            ywin_ref[e * MOE_WIN:(e + 1) * MOE_WIN, :] = y_ref[0, pl.ds(e * cap + first, MOE_WIN), :]
            first1_cols = jnp.where(jnp.right_shift(col, win_shift) == e, (first + 1).astype(F32), first1_cols)
        hit = slot1_cols - first1_cols == jnp.bitwise_and(col, MOE_WIN - 1).astype(F32)
        place = jnp.where(hit, gate_cols, 0.0).astype(BF16)
        o_ref[0, toks, :] = x1_ref[0, toks, :] + _dot(place, ywin_ref[...])

    def full_scan(toks):
        acc = x1_ref[0, toks, :]
        slot_t = slot_t_ref[0, toks, :]
        gate_t = gate_t_ref[0, toks, :]
        slot_ids = lax.broadcasted_iota(I32, (tc, cap), 1)
        for e in range(N_EXPERTS):
            place = jnp.where(slot_ids == slot_t[:, e:e + 1], gate_t[:, e:e + 1], 0.0).astype(BF16)
            acc = acc + _dot(place, y_ref[0, e * cap:(e + 1) * cap, :])
        o_ref[0, toks, :] = acc

    for k in range(MOE_STEP_CHUNKS):
        toks = slice(k * tc, (k + 1) * tc)
        fits = meta_ref[0, k, 0, META_FLAG_LANE] == 0
        pl.when(fits)(functools.partial(fast, k, toks))
        pl.when(jnp.logical_not(fits))(functools.partial(full_scan, toks))


def _combine_call(meta, x1, slot_t, gate_t, y, cap):
    b, s, d = x1.shape
    tc = MOE_TC * MOE_STEP_CHUNKS
    return pl.pallas_call(
        functools.partial(_combine_kernel, cap=cap),
        grid=(b, s // tc),
        in_specs=[_meta_spec(),
                  pl.BlockSpec((1, tc, d), lambda bi, ci: (bi, ci, 0)),
                  pl.BlockSpec((1, tc, LANES), lambda bi, ci: (bi, ci, 0)),
                  pl.BlockSpec((1, tc, LANES), lambda bi, ci: (bi, ci, 0)),
                  pl.BlockSpec((1, N_EXPERTS * cap, d), lambda bi, ci: (bi, 0, 0))],
        out_specs=pl.BlockSpec((1, tc, d), lambda bi, ci: (bi, ci, 0)),
        out_shape=jax.ShapeDtypeStruct((b, s, d), F32),
        scratch_shapes=[pltpu.VMEM((N_EXPERTS * MOE_WIN, d), BF16)],
        compiler_params=pltpu.CompilerParams(
            dimension_semantics=("arbitrary", "arbitrary"), vmem_limit_bytes=VMEM_LIMIT),
        name="combine",
    )(meta, x1, slot_t, gate_t, y)


def _rope_tables(n, rot_dim, lane_lo, lane_hi, period):
    rows = n // GRID_W
    row = jnp.broadcast_to(jnp.arange(rows)[:, None], (rows, GRID_W)).reshape(n).astype(F32)
    col = jnp.broadcast_to(jnp.arange(GRID_W)[None, :], (rows, GRID_W)).reshape(n).astype(F32)
    nf = rot_dim // 4
    inv = ROPE_THETA ** (-jnp.arange(nf, dtype=F32) / nf)
    ang = jnp.concatenate([row[:, None] * inv, col[:, None] * inv], axis=-1)
    cos, sin = jnp.cos(ang), jnp.sin(ang)
    lane = np.arange(LANES)
    active = (lane >= lane_lo) & (lane < lane_hi)
    pair = np.where(active, ((lane - lane_lo) % period) // 2, 0)
    even = (lane % 2) == 0
    c = jnp.where(active[None, :], cos[:, pair], 1.0)
    sa = jnp.where((active & even)[None, :], -sin[:, pair], 0.0)
    sb = jnp.where((active & ~even)[None, :], sin[:, pair], 0.0)
    return c, sa, sb


def _pad_cols(w, heads, width):
    k = w.shape[0]
    w = w.reshape(k, heads, width)
    return jnp.pad(w, ((0, 0), (0, 0), (0, LANES - width))).reshape(k, heads * LANES)


def kernel(x, g_attn_norm, w_in, b_gate, g_q_lat, w_q_up, g_kv_lat, w_kv_up, g_mla_qnorm, g_mla_knorm, g_gqa_qnorm, g_gqa_knorm, w_mla_branch, w_gqa_branch, w_out, g_ffn_norm, w_router, b_router, w_exp_gate, w_exp_up, w_exp_down):
    b, s, d = x.shape
    assert d == D_MODEL and s % max(PRE_TM, ROUTE_BLK) == 0 and ROUTE_BLK % MOE_TC == 0
    assert s % (MOE_TC * MOE_STEP_CHUNKS) == 0 and (b * s) % POST_TM == 0
    cap = CAPACITY_FACTOR * s // N_EXPERTS
    assert cap >= MOE_WIN and cap % SLOT_ALIGN == 0
    t = b * s

    splits = np.cumsum([Q_LORA, KV_LORA, MLA_ROPE, GQA_HEADS * GQA_HEAD_DIM,
                        GQA_KV_HEADS * GQA_HEAD_DIM, GQA_KV_HEADS * GQA_HEAD_DIM, D_MODEL])
    w_ql, w_kvl, w_kr, w_qg, w_kg, w_vg, w_ga, w_gb = jnp.split(w_in, splits, axis=1)
    w_kr = jnp.pad(w_kr, ((0, 0), (MLA_NOPE, LANES - MLA_QK)))
    w_kg = w_kg.reshape(d, GQA_KV_HEADS, 1, GQA_HEAD_DIM)
    w_kg = jnp.concatenate([w_kg, w_kg], axis=2).reshape(d, GQA_KV_HEADS * LANES)
    w_vg = w_vg.reshape(d, GQA_KV_HEADS, 1, GQA_HEAD_DIM)
    zeros = jnp.zeros_like(w_vg)
    w_vg = jnp.concatenate([jnp.concatenate([w_vg, zeros], axis=3),
                            jnp.concatenate([zeros, w_vg], axis=3)], axis=2).reshape(d, -1)
    win = jnp.concatenate([w_ql, w_kvl, w_kr, w_qg, w_kg, w_vg, w_ga, w_gb], axis=1).astype(BF16)
    assert win.shape[1] == _C_END
    wqup = _pad_cols(w_q_up, MLA_HEADS, MLA_QK).astype(BF16)
    v_even = w_kv_up.reshape(KV_LORA, MLA_HEADS, LANES)[:, 0::2, MLA_NOPE:]
    wkvup = jnp.concatenate([w_kv_up, _pad_cols(v_even.reshape(KV_LORA, -1), MLA_HEADS // 2, MLA_V)],
                            axis=1).astype(BF16)

    def row(v):
        return v.reshape(1, -1).astype(F32)

    gq = row(jnp.pad(g_mla_qnorm, (0, LANES - MLA_QK)))
    gk = row(jnp.pad(g_mla_knorm, (0, LANES - MLA_QK)))
    ggq = row(jnp.tile(g_gqa_qnorm, 2))
    ggk = row(jnp.tile(g_gqa_knorm, 2))

    assert MLA_V == HALF and GQA_HEAD_DIM == HALF
    wa = w_mla_branch.astype(BF16)
    wb = w_gqa_branch.astype(BF16)

    wr = jnp.pad(w_router, ((0, 0), (0, LANES - N_EXPERTS)))
    wr_hi = wr.astype(BF16)
    wr_lo = (wr - wr_hi.astype(F32)).astype(BF16)
    br = jnp.concatenate([b_router.astype(F32), jnp.full((LANES - N_EXPERTS,), -1e30, F32)]).reshape(1, LANES)

    tabs_a = _rope_tables(s, MLA_ROPE, MLA_NOPE, MLA_QK, MLA_ROPE)
    tabs_b = _rope_tables(s, GQA_HEAD_DIM, 0, LANES, GQA_HEAD_DIM)

    x2d = x.reshape(t, d)
    qa, ka, va, qb, kb, vb, ga, gb = _pre_call(
        x2d, s, row(g_attn_norm), win, row(g_q_lat), wqup, row(g_kv_lat), wkvup,
        gk, ggk, row(b_gate), tabs_a + tabs_b)

    def seq(a):
        return a.reshape(b, s, a.shape[-1])

    ya = _flash_call(seq(qa), gq, tabs_a, seq(ka), seq(va), MLA_HEADS, 2, ((0, None), (1, None)),
                     (0, 1), (0, 1), MLA_QK, "attention_mla")
    yb = _flash_call(seq(qb), ggq, tabs_b, seq(kb), seq(vb), GQA_HEADS, GQA_GROUP,
                     tuple((g // 2, g % 2) for g in range(GQA_GROUP)), (0,) * GQA_GROUP,
                     tuple(g % 2 for g in range(GQA_GROUP)), GQA_HEAD_DIM, "attention_gqa")

    x1, h2, aff, aff_rows = _post_call(x2d, ya.reshape(t, -1), yb.reshape(t, -1), ga, gb,
                                       wa, wb, w_out.astype(BF16), row(g_ffn_norm), wr_hi, wr_lo, br)

    slot_t, gate_t, slot_r, meta = _route_call(seq(aff), aff_rows, cap)
    meta = meta.reshape(b, s // MOE_TC, 1, LANES)
    xe = _dispatch_call(meta, slot_r, seq(h2), cap)
    y = _expert_call(xe, w_exp_gate, w_exp_up, w_exp_down, cap)
    return _combine_call(meta, seq(x1), slot_t, gate_t, y, cap)
```

```python
import functools

import jax
import jax.numpy as jnp
import numpy as np
from jax import lax
from jax.experimental import pallas as pl
from jax.experimental.pallas import tpu as pltpu

F32 = jnp.float32
BF16 = jnp.bfloat16
I32 = jnp.int32

D_MODEL = 1024
GRID_W = 64
ROPE_THETA = 10000.0
EPS = 1e-6

MLA_HEADS = 8
MLA_NOPE = 64
MLA_ROPE = 32
MLA_QK = MLA_NOPE + MLA_ROPE
MLA_V = 64
Q_LORA = 768
KV_LORA = 256

GQA_HEADS = 8
GQA_KV_HEADS = 2
GQA_HEAD_DIM = 64
GQA_GROUP = GQA_HEADS // GQA_KV_HEADS

N_EXPERTS = 16
CAPACITY_FACTOR = 2
EXPERT_FF = 1024

MIN_NORMAL_BITS = 0x00800000
LANES = 128
HEADS_WIDTH = MLA_HEADS * LANES

_C_QLAT = 0
_C_KVLAT = _C_QLAT + Q_LORA
_C_KROPE = _C_KVLAT + KV_LORA
_C_QG = _C_KROPE + LANES
_C_KG = _C_QG + GQA_HEADS * GQA_HEAD_DIM
_C_VG = _C_KG + GQA_KV_HEADS * LANES
_C_GA = _C_VG + 2 * GQA_KV_HEADS * LANES
_C_GB = _C_GA + D_MODEL
_C_END = _C_GB + D_MODEL

HALF = LANES // 2
LOG2_E = 1.4426950408889634

PRE_TM = 256
POST_TM = 1024
EXPERT_SEQS = 2
ATT_TQ = 256
ATT_UNITS = 16
ATT_TK = 512
ATT_TK2 = 1024
ROUTE_BLK = 512
MOE_TC = 256
MOE_STEP_CHUNKS = 4
MOE_WIN = 64
SLOT_ALIGN = 16
META_FLAG_LANE = N_EXPERTS
VMEM_LIMIT = 56 * 1024 * 1024

_NT = (((1,), (1,)), ((), ()))


def _dot(a, b):
    return jnp.dot(a, b, preferred_element_type=F32)


def _rms(x, width):
    return x * lax.rsqrt(jnp.sum(x * x, axis=-1, keepdims=True) * (1.0 / width) + EPS)


def _value_block(x, lane, half):
    keep = (lane >= half * HALF) & (lane < (half + 1) * HALF)
    return jnp.where(lane == _ones_lane(half), 1.0, jnp.where(keep, x, 0.0))


def _ones_lane(out_half):
    return (1 - out_half) * HALF


def _rope(x, c, sa, sb):
    return x * c + pltpu.roll(x, LANES - 1, 1) * sa + pltpu.roll(x, 1, 1) * sb


def _pre_kernel(x_ref, gattn_ref, win_ref, gql_ref, wqup_ref, gkvl_ref, wkvup_ref,
                gk_ref, ggk_ref, bgate_ref,
                cm_ref, sam_ref, sbm_ref, cg_ref, sag_ref, sbg_ref,
                qa_ref, ka_ref, va_ref, qb_ref, kb_ref, vb_ref, ga_ref, gb_ref):
    x = x_ref[...]
    h = (_rms(x, D_MODEL) * gattn_ref[...]).astype(BF16)

    def proj(lo, hi):
        return _dot(h, win_ref[:, lo:hi])

    lane = lax.broadcasted_iota(I32, (x.shape[0], LANES), 1)
    low_half = lane < GQA_HEAD_DIM
    cm, sam, sbm = cm_ref[...], sam_ref[...], sbm_ref[...]
    cg, sag, sbg = cg_ref[...], sag_ref[...], sbg_ref[...]

    latents = proj(_C_QLAT, _C_QG)
    c_q = (_rms(latents[:, _C_QLAT:_C_KVLAT], Q_LORA) * gql_ref[...]).astype(BF16)
    qa_ref[...] = _dot(c_q, wqup_ref[...])

    c_kv = (_rms(latents[:, _C_KVLAT:_C_KROPE], KV_LORA) * gkvl_ref[...]).astype(BF16)
    k_rope = latents[:, _C_KROPE:_C_QG]
    kv_up = _dot(c_kv, wkvup_ref[...])
    gk = gk_ref[...]
    for hd in range(MLA_HEADS):
        kv = kv_up[:, hd * LANES:(hd + 1) * LANES]
        kseg = jnp.where(low_half, kv, 0.0) + k_rope
        kseg = _rms(kseg, MLA_QK) * gk
        ka_ref[:, hd * LANES:(hd + 1) * LANES] = _rope(kseg, cm, sam, sbm).astype(BF16)
        if hd % 2 == 0:
            kv = kv_up[:, (MLA_HEADS + hd // 2) * LANES:(MLA_HEADS + hd // 2 + 1) * LANES]
        va_ref[:, hd * LANES:(hd + 1) * LANES] = _value_block(kv, lane, hd % 2).astype(BF16)

    qb_ref[...] = proj(_C_QG, _C_KG)
    ggk = ggk_ref[...]
    kvg = proj(_C_KG, _C_GA)
    kv_w = GQA_KV_HEADS * LANES
    for hd in range(GQA_KV_HEADS):
        seg = _rms(kvg[:, hd * LANES:(hd + 1) * LANES], LANES) * ggk
        kb_ref[:, hd * LANES:(hd + 1) * LANES] = _rope(seg, cg, sag, sbg).astype(BF16)
        for half in range(2):
            blk = 2 * hd + half
            vraw = kvg[:, kv_w + blk * LANES:kv_w + (blk + 1) * LANES]
            vb_ref[:, blk * LANES:(blk + 1) * LANES] = _value_block(vraw, lane, half).astype(BF16)

    gates = jax.nn.sigmoid(proj(_C_GA, _C_END) + bgate_ref[...])
    ga_ref[...] = gates[:, :D_MODEL].astype(BF16)
    gb_ref[...] = gates[:, D_MODEL:].astype(BF16)


def _pre_call(x2d, seq, gattn, win, gql, wqup, gkvl, wkvup, gk, ggk, bgate, tabs):
    t = x2d.shape[0]
    tm = PRE_TM
    n_seq_tiles = seq // tm

    def full(a):
        return pl.BlockSpec(a.shape, lambda i: (0,) * a.ndim, pipeline_mode=pl.Buffered(1))

    def tok(width):
        return pl.BlockSpec((tm, width), lambda i: (i, 0))

    tab_spec = pl.BlockSpec((tm, LANES), lambda i: (i % n_seq_tiles, 0))
    params = (gattn, win, gql, wqup, gkvl, wkvup, gk, ggk, bgate)
    kv_w = GQA_KV_HEADS * LANES
    outs = ((HEADS_WIDTH, F32), (HEADS_WIDTH, BF16), (HEADS_WIDTH, BF16), (GQA_HEADS * GQA_HEAD_DIM, F32),
            (kv_w, BF16), (2 * kv_w, BF16), (D_MODEL, BF16), (D_MODEL, BF16))
    return pl.pallas_call(
        _pre_kernel,
        grid=(t // tm,),
        in_specs=[tok(D_MODEL)] + [full(a) for a in params] + [tab_spec] * 6,
        out_specs=[tok(w) for w, _ in outs],
        out_shape=[jax.ShapeDtypeStruct((t, w), dt) for w, dt in outs],
        compiler_params=pltpu.CompilerParams(
            dimension_semantics=("arbitrary",), vmem_limit_bytes=VMEM_LIMIT),
        name="pre_attention",
    )(x2d, *params, *tabs)


def _flash_kernel(q_ref, gain_ref, c_ref, sa_ref, sb_ref, k_ref, v_ref, o_ref, s0_ref, s1_ref, *,
                  units, q_block, k_block, v_block, head_dim):
    tq = s0_ref.shape[0]
    s_len = k_ref.shape[1]
    s_bufs = (s0_ref, s1_ref)
    q_scale = head_dim ** -0.5 * LOG2_E
    lane = lax.broadcasted_iota(I32, (tq, LANES), 1)

    def lanes_of(block):
        return slice(block * LANES, (block + 1) * LANES)

    def q_of(u):
        r, g = units[u]
        rows = slice(r * tq, (r + 1) * tq)
        block, half = q_block[g]
        q = q_ref[0, rows, lanes_of(block)]
        if half is not None:
            q = jnp.where((lane >= half * HALF) & (lane < (half + 1) * HALF), q, 0.0)
        q = _rms(q, head_dim) * gain_ref[...]
        q = _rope(q, c_ref[rows, :], sa_ref[rows, :], sb_ref[rows, :]) * q_scale
        return q.astype(BF16)

    def score_pass(u, j, q, m128):
        cols = slice(j * ATT_TK, (j + 1) * ATT_TK)
        keys = k_ref[0, cols, lanes_of(k_block[units[u][1]])]
        s = lax.dot_general(q, keys, _NT, preferred_element_type=F32)
        s_bufs[u % 2][:, cols] = s
        for c in range(ATT_TK // LANES):
            m128 = jnp.maximum(m128, s[:, c * LANES:(c + 1) * LANES])
        return m128

    def value_pass(u, j, m, acc):
        cols = slice(j * ATT_TK2, (j + 1) * ATT_TK2)
        p = jnp.exp2(s_bufs[u % 2][:, cols] - m)
        return acc + _dot(p.astype(BF16), v_ref[0, cols, lanes_of(v_block[units[u][1]])])

    neg = jnp.full((tq, LANES), -1e30, F32)
    m128 = neg
    q_first = q_of(0)
    for j in range(s_len // ATT_TK):
        m128 = score_pass(0, j, q_first, m128)
    ratio = ATT_TK2 // ATT_TK
    for u in range(len(units)):
        m = jnp.max(m128, axis=-1, keepdims=True)
        acc = jnp.zeros((tq, LANES), F32)
        has_next = u + 1 < len(units)
        if has_next:
            q_next = q_of(u + 1)
            m128 = neg
        for j2 in range(s_len // ATT_TK2):
            acc = value_pass(u, j2, m, acc)
            if has_next:
                for j in range(j2 * ratio, (j2 + 1) * ratio):
                    m128 = score_pass(u + 1, j, q_next, m128)
        r, g = units[u]
        ones = _ones_lane(g % 2)
        out = acc / acc[:, ones:ones + 1]
        if g % 2 == 0:
            out_even = out
        else:
            pair = jnp.where(lane < HALF, out_even, out)
            o_ref[0, r * tq:(r + 1) * tq, lanes_of(g // 2)] = pair.astype(BF16)


def _flash_call(q, gain, tabs, k, v, n_heads, heads, q_block, k_block, v_block, head_dim, name):
    b, s, qw = q.shape
    n_steps = n_heads // heads
    qw_step = qw // n_steps
    kw = k.shape[2] // n_steps
    vw = v.shape[2] // n_steps
    assert heads % 2 == 0 and ATT_UNITS % heads == 0
    assert len(q_block) == len(k_block) == len(v_block) == heads
    assert s % ATT_TK2 == 0 and ATT_TK2 % ATT_TK == 0
    row_tiles = ATT_UNITS // heads
    rows = ATT_TQ * row_tiles
    assert s % rows == 0
    units = tuple((r, g) for r in range(row_tiles) for g in range(heads))
    q_spec = pl.BlockSpec((1, rows, qw_step), lambda bi, kh, qi: (bi, qi, kh))
    o_spec = pl.BlockSpec((1, rows, heads * HALF), lambda bi, kh, qi: (bi, qi, kh))
    k_spec = pl.BlockSpec((1, s, kw), lambda bi, kh, qi: (bi, 0, kh))
    v_spec = pl.BlockSpec((1, s, vw), lambda bi, kh, qi: (bi, 0, kh))
    tab_spec = pl.BlockSpec((rows, LANES), lambda bi, kh, qi: (qi, 0))
    gain_spec = pl.BlockSpec((1, LANES), lambda bi, kh, qi: (0, 0))
    return pl.pallas_call(
        functools.partial(_flash_kernel, units=units, q_block=q_block, k_block=k_block, v_block=v_block,
                          head_dim=head_dim),
        grid=(b, n_steps, s // rows),
        in_specs=[q_spec, gain_spec, tab_spec, tab_spec, tab_spec, k_spec, v_spec],
        out_specs=o_spec,
        out_shape=jax.ShapeDtypeStruct((b, s, n_heads * HALF), BF16),
        scratch_shapes=[pltpu.VMEM((ATT_TQ, s), F32), pltpu.VMEM((ATT_TQ, s), F32)],
        compiler_params=pltpu.CompilerParams(
            dimension_semantics=("arbitrary", "arbitrary", "arbitrary"),
            vmem_limit_bytes=VMEM_LIMIT),
        name=name,
    )(q, gain, *tabs, k, v)


def _post_kernel(x_ref, ya_ref, yb_ref, ga_ref, gb_ref, wa_ref, wb_ref, wout_ref,
                 gffn_ref, wrh_ref, wrl_ref, br_ref, x1_ref, h2_ref, aff_ref, aff_rows_ref):
    a = _dot(ya_ref[...], wa_ref[...])
    b = _dot(yb_ref[...], wb_ref[...])
    merged = ga_ref[...].astype(F32) * a + gb_ref[...].astype(F32) * b
    x1 = x_ref[...] + _dot(merged.astype(BF16), wout_ref[...])
    x1_ref[...] = x1
    h2 = _rms(x1, D_MODEL) * gffn_ref[...]
    h2_hi = h2.astype(BF16)
    h2_ref[...] = h2_hi
    h2_lo = (h2 - h2_hi.astype(F32)).astype(BF16)
    wrh = wrh_ref[...]
    logits = _dot(h2_hi, wrh) + _dot(h2_lo, wrh) + _dot(h2_hi, wrl_ref[...]) + br_ref[...]
    e = jnp.exp(logits - jnp.max(logits, axis=-1, keepdims=True))
    aff = e / jnp.sum(e, axis=-1, keepdims=True)
    aff_ref[...] = aff
    aff_rows_ref[...] = jnp.transpose(aff)[:N_EXPERTS]


def _post_call(x2d, ya, yb, ga, gb, wa, wb, wout, gffn, wrh, wrl, br):
    t = x2d.shape[0]
    tm = POST_TM

    def full(a):
        return pl.BlockSpec(a.shape, lambda i: (0,) * a.ndim, pipeline_mode=pl.Buffered(1))

    def tok(width):
        return pl.BlockSpec((tm, width), lambda i: (i, 0))

    params = (wa, wb, wout, gffn, wrh, wrl, br)
    return pl.pallas_call(
        _post_kernel,
        grid=(t // tm,),
        in_specs=[tok(a.shape[1]) for a in (x2d, ya, yb, ga, gb)] + [full(a) for a in params],
        out_specs=[tok(D_MODEL), tok(D_MODEL), tok(LANES),
                   pl.BlockSpec((N_EXPERTS, tm), lambda i: (0, i))],
        out_shape=[jax.ShapeDtypeStruct((t, D_MODEL), F32),
                   jax.ShapeDtypeStruct((t, D_MODEL), BF16),
                   jax.ShapeDtypeStruct((t, LANES), F32),
                   jax.ShapeDtypeStruct((N_EXPERTS, t), F32)],
        compiler_params=pltpu.CompilerParams(
            dimension_semantics=("arbitrary",), vmem_limit_bytes=VMEM_LIMIT),
        name="post_attention",
    )(x2d, ya, yb, ga, gb, *params)


def _window_meta(start, end, cap):
    lane = lax.broadcasted_iota(I32, start.shape, 1)
    first = jnp.minimum(jnp.floor(start * (1.0 / SLOT_ALIGN)) * SLOT_ALIGN, float(cap - MOE_WIN))
    over = jnp.where((lane < N_EXPERTS) & (end - first > MOE_WIN), 1.0, 0.0)
    flag = jnp.max(over, axis=-1, keepdims=True)
    return jnp.where(lane == META_FLAG_LANE, flag, first).astype(I32)


def _route_kernel(aff_ref, aff_rows_ref, slot_t_ref, gate_t_ref, slot_r_ref, meta_ref, *, cap):
    s = aff_ref.shape[1]

    def count(mask):
        return jnp.sum(jnp.where(mask, 1.0, 0.0), axis=1, keepdims=True)

    def as_float(bits):
        return lax.bitcast_convert_type(bits, F32)

    def search(i, bits):
        cand = bits | jnp.left_shift(jnp.int32(1), 29 - i)
        return jnp.where(count(aff_rows_ref[...] >= as_float(cand)) >= cap, cand, bits)

    bits = lax.fori_loop(0, 30, search, jnp.zeros((N_EXPERTS, 1), I32))
    is_normal = bits >= MIN_NORMAL_BITS
    thr_col = jnp.where(is_normal, as_float(bits), 0.0)
    next_col = as_float(jnp.where(is_normal, bits + 1, MIN_NORMAL_BITS))
    need_col = cap - count(aff_rows_ref[...] >= next_col)

    def to_lanes(col):
        on_diag = (lax.broadcasted_iota(I32, (N_EXPERTS, LANES), 0)
                   == lax.broadcasted_iota(I32, (N_EXPERTS, LANES), 1))
        return jnp.sum(jnp.where(on_diag, col, 0.0), axis=0, keepdims=True)

    thr, thr_next, need = to_lanes(thr_col), to_lanes(next_col), to_lanes(need_col)

    blk = ROUTE_BLK
    tri = (lax.broadcasted_iota(I32, (blk, blk), 0) >= lax.broadcasted_iota(I32, (blk, blk), 1)).astype(BF16)
    eye = (lax.broadcasted_iota(I32, (LANES, LANES), 0) == lax.broadcasted_iota(I32, (LANES, LANES), 1)).astype(BF16)
    carry_eq = jnp.zeros((1, LANES), F32)
    carry_sel = jnp.zeros((1, LANES), F32)
    for c in range(s // blk):
        rows = slice(c * blk, (c + 1) * blk)
        ab = aff_ref[0, rows, :]
        gt = ab >= thr_next
        eq = jnp.where((ab >= thr) & jnp.logical_not(gt), 1.0, 0.0)
        eq_incl = _dot(tri, eq.astype(BF16)) + carry_eq
        carry_eq = eq_incl[blk - 1:blk, :]
        sel = jnp.where(gt | ((eq > 0.0) & (eq_incl - eq < need)), 1.0, 0.0)
        sel_incl = _dot(tri, sel.astype(BF16)) + carry_sel
        edge = carry_sel
        for k in range(blk // MOE_TC):
            nxt = sel_incl[(k + 1) * MOE_TC - 1:(k + 1) * MOE_TC, :]
            chunk = c * (blk // MOE_TC) + k
            meta_ref[0, chunk:chunk + 1, :] = _window_meta(edge, nxt, cap)
            edge = nxt
        carry_sel = sel_incl[blk - 1:blk, :]
        slot1 = sel * sel_incl
        slot_t_ref[0, rows, :] = slot1.astype(I32) - 1
        gate_t_ref[0, rows, :] = sel * aff_ref[0, rows, :]
        hi = jnp.floor(slot1 * (1.0 / 32.0))
        lo = slot1 - 32.0 * hi
        rows_hi = lax.dot_general(eye, hi.astype(BF16), _NT, preferred_element_type=F32)
        rows_lo = lax.dot_general(eye, lo.astype(BF16), _NT, preferred_element_type=F32)
        slot_rows = (32.0 * rows_hi + rows_lo).astype(I32) - 1
        slot_r_ref[0, :, rows] = slot_rows[:N_EXPERTS]


def _route_call(aff, aff_rows, cap):
    b, s, _ = aff.shape
    tok_spec = pl.BlockSpec((1, s, LANES), lambda bi: (bi, 0, 0))
    return pl.pallas_call(
        functools.partial(_route_kernel, cap=cap),
        grid=(b,),
        in_specs=[tok_spec, pl.BlockSpec((N_EXPERTS, s), lambda bi: (0, bi))],
        out_specs=[tok_spec, tok_spec, pl.BlockSpec((1, N_EXPERTS, s), lambda bi: (bi, 0, 0)),
                   pl.BlockSpec((1, s // MOE_TC, LANES), lambda bi: (bi, 0, 0))],
        out_shape=[jax.ShapeDtypeStruct((b, s, LANES), I32),
                   jax.ShapeDtypeStruct((b, s, LANES), F32),
                   jax.ShapeDtypeStruct((b, N_EXPERTS, s), I32),
                   jax.ShapeDtypeStruct((b, s // MOE_TC, LANES), I32)],
        compiler_params=pltpu.CompilerParams(
            dimension_semantics=("arbitrary",), vmem_limit_bytes=VMEM_LIMIT),
        name="route",
    )(aff, aff_rows)


def _meta_spec():
    return pl.BlockSpec((1, MOE_STEP_CHUNKS, 1, LANES), lambda bi, ci: (bi, ci, 0, 0),
                        memory_space=pltpu.SMEM)


def _dispatch_kernel(meta_ref, slot_r_ref, h2_ref, xe_ref, *, cap):
    tc = MOE_TC

    @pl.when(pl.program_id(1) == 0)
    def _():
        xe_ref[...] = jnp.zeros_like(xe_ref)

    def add_rows(first_row, rows):
        n = rows.shape[0]
        cur = xe_ref[0, pl.ds(first_row, n), :].astype(F32)
        xe_ref[0, pl.ds(first_row, n), :] = (cur + rows).astype(BF16)

    for k in range(MOE_STEP_CHUNKS):
        toks = slice(k * tc, (k + 1) * tc)
        slot_rows = slot_r_ref[0, :, toks]
        h2c = h2_ref[0, toks, :]
        fits = meta_ref[0, k, 0, META_FLAG_LANE] == 0

        @pl.when(fits)
        def _(k=k, slot_rows=slot_rows, h2c=h2c):
            firsts = [pl.multiple_of(meta_ref[0, k, 0, e], SLOT_ALIGN) for e in range(N_EXPERTS)]
            win_ids = lax.broadcasted_iota(I32, (MOE_WIN, tc), 0)
            half = N_EXPERTS // 2
            for lo in (0, half):
                pick = jnp.concatenate(
                    [jnp.where(win_ids == slot_rows[e:e + 1, :] - firsts[e], 1.0, 0.0).astype(BF16)
                     for e in range(lo, lo + half)], axis=0)
                got = _dot(pick, h2c)
                for i, e in enumerate(range(lo, lo + half)):
                    add_rows(e * cap + firsts[e], got[i * MOE_WIN:(i + 1) * MOE_WIN])

        @pl.when(jnp.logical_not(fits))
        def _(slot_rows=slot_rows, h2c=h2c):
            slot_ids = lax.broadcasted_iota(I32, (cap, tc), 0)
            for e in range(N_EXPERTS):
                pick = jnp.where(slot_ids == slot_rows[e:e + 1, :], 1.0, 0.0).astype(BF16)
                add_rows(e * cap, _dot(pick, h2c))


def _dispatch_call(meta, slot_r, h2, cap):
    b, s, d = h2.shape
    tc = MOE_TC * MOE_STEP_CHUNKS
    return pl.pallas_call(
        functools.partial(_dispatch_kernel, cap=cap),
        grid=(b, s // tc),
        in_specs=[_meta_spec(),
                  pl.BlockSpec((1, N_EXPERTS, tc), lambda bi, ci: (bi, 0, ci)),
                  pl.BlockSpec((1, tc, d), lambda bi, ci: (bi, ci, 0))],
        out_specs=pl.BlockSpec((1, N_EXPERTS * cap, d), lambda bi, ci: (bi, 0, 0)),
        out_shape=jax.ShapeDtypeStruct((b, N_EXPERTS * cap, d), BF16),
        compiler_params=pltpu.CompilerParams(
            dimension_semantics=("arbitrary", "arbitrary"), vmem_limit_bytes=VMEM_LIMIT),
        name="dispatch",
    )(meta, slot_r, h2)


def _expert_kernel(xe_ref, wg_ref, wu_ref, wd_ref, y_ref, wg_bf, wu_bf, wd_bf):
    @pl.when(pl.program_id(1) == 0)
    def _():
        wg_bf[...] = wg_ref[0].astype(BF16)
        wu_bf[...] = wu_ref[0].astype(BF16)
        wd_bf[...] = wd_ref[0].astype(BF16)

    nseq, cap, d = xe_ref.shape
    xe = xe_ref[...].reshape(nseq * cap, d)
    a = _dot(xe, wg_bf[...])
    u = _dot(xe, wu_bf[...])
    act = (a * jax.nn.sigmoid(a) * u).astype(BF16)
    y_ref[...] = _dot(act, wd_bf[...]).astype(BF16).reshape(nseq, cap, d)


def _expert_call(xe, wg, wu, wd, cap):
    b, _, d = xe.shape
    nseq = EXPERT_SEQS if b % EXPERT_SEQS == 0 else 1
    up_spec = pl.BlockSpec((1, d, EXPERT_FF), lambda e, bi: (e, 0, 0))
    down_spec = pl.BlockSpec((1, EXPERT_FF, d), lambda e, bi: (e, 0, 0))
    tok_spec = pl.BlockSpec((nseq, cap, d), lambda e, bi: (bi, e, 0))
    return pl.pallas_call(
        _expert_kernel,
        grid=(N_EXPERTS, b // nseq),
        in_specs=[tok_spec, up_spec, up_spec, down_spec],
        out_specs=tok_spec,
        out_shape=jax.ShapeDtypeStruct(xe.shape, BF16),
        scratch_shapes=[pltpu.VMEM((d, EXPERT_FF), BF16), pltpu.VMEM((d, EXPERT_FF), BF16),
                        pltpu.VMEM((EXPERT_FF, d), BF16)],
        compiler_params=pltpu.CompilerParams(
            dimension_semantics=("arbitrary", "arbitrary"), vmem_limit_bytes=VMEM_LIMIT),
        name="experts",
    )(xe, wg, wu, wd)


def _combine_kernel(meta_ref, x1_ref, slot_t_ref, gate_t_ref, y_ref, o_ref, ywin_ref, *, cap):
    tc = MOE_TC
    cols = N_EXPERTS * MOE_WIN
    win_shift = MOE_WIN.bit_length() - 1

    def fast(k, toks):
        col_expert = jnp.right_shift(lax.broadcasted_iota(I32, (LANES, cols), 1), win_shift)
        spread = jnp.where(lax.broadcasted_iota(I32, (LANES, cols), 0) == col_expert, 1.0, 0.0).astype(BF16)
        slot1 = (slot_t_ref[0, toks, :] + 1).astype(F32)
        hi = jnp.floor(slot1 * (1.0 / 32.0))
        lo = slot1 - 32.0 * hi
        slot1_cols = 32.0 * _dot(hi.astype(BF16), spread) + _dot(lo.astype(BF16), spread)
        gate_cols = _dot(gate_t_ref[0, toks, :].astype(BF16), spread)
        col = lax.broadcasted_iota(I32, (1, cols), 1)
        first1_cols = jnp.zeros((1, cols), F32)
        for e in range(N_EXPERTS):
            first = pl.multiple_of(meta_ref[0, k, 0, e], SLOT_ALIGN)
            ywin_ref[e * MOE_WIN:(e + 1) * MOE_WIN, :] = y_ref[0, pl.ds(e * cap + first, MOE_WIN), :]
            first1_cols = jnp.where(jnp.right_shift(col, win_shift) == e, (first + 1).astype(F32), first1_cols)
        hit = slot1_cols - first1_cols == jnp.bitwise_and(col, MOE_WIN - 1).astype(F32)
        place = jnp.where(hit, gate_cols, 0.0).astype(BF16)
        o_ref[0, toks, :] = x1_ref[0, toks, :] + _dot(place, ywin_ref[...])

    def full_scan(toks):
        acc = x1_ref[0, toks, :]
        slot_t = slot_t_ref[0, toks, :]
        gate_t = gate_t_ref[0, toks, :]
        slot_ids = lax.broadcasted_iota(I32, (tc, cap), 1)
        for e in range(N_EXPERTS):
            place = jnp.where(slot_ids == slot_t[:, e:e + 1], gate_t[:, e:e + 1], 0.0).astype(BF16)
            acc = acc + _dot(place, y_ref[0, e * cap:(e + 1) * cap, :])
        o_ref[0, toks, :] = acc

    for k in range(MOE_STEP_CHUNKS):
        toks = slice(k * tc, (k + 1) * tc)
        fits = meta_ref[0, k, 0, META_FLAG_LANE] == 0
        pl.when(fits)(functools.partial(fast, k, toks))
        pl.when(jnp.logical_not(fits))(functools.partial(full_scan, toks))


def _combine_call(meta, x1, slot_t, gate_t, y, cap):
    b, s, d = x1.shape
    tc = MOE_TC * MOE_STEP_CHUNKS
    return pl.pallas_call(
        functools.partial(_combine_kernel, cap=cap),
        grid=(b, s // tc),
        in_specs=[_meta_spec(),
                  pl.BlockSpec((1, tc, d), lambda bi, ci: (bi, ci, 0)),
                  pl.BlockSpec((1, tc, LANES), lambda bi, ci: (bi, ci, 0)),
                  pl.BlockSpec((1, tc, LANES), lambda bi, ci: (bi, ci, 0)),
                  pl.BlockSpec((1, N_EXPERTS * cap, d), lambda bi, ci: (bi, 0, 0))],
        out_specs=pl.BlockSpec((1, tc, d), lambda bi, ci: (bi, ci, 0)),
        out_shape=jax.ShapeDtypeStruct((b, s, d), F32),
        scratch_shapes=[pltpu.VMEM((N_EXPERTS * MOE_WIN, d), BF16)],
        compiler_params=pltpu.CompilerParams(
            dimension_semantics=("arbitrary", "arbitrary"), vmem_limit_bytes=VMEM_LIMIT),
        name="combine",
    )(meta, x1, slot_t, gate_t, y)


def _rope_tables(n, rot_dim, lane_lo, lane_hi, period):
    rows = n // GRID_W
    row = jnp.broadcast_to(jnp.arange(rows)[:, None], (rows, GRID_W)).reshape(n).astype(F32)
    col = jnp.broadcast_to(jnp.arange(GRID_W)[None, :], (rows, GRID_W)).reshape(n).astype(F32)
    nf = rot_dim // 4
    inv = ROPE_THETA ** (-jnp.arange(nf, dtype=F32) / nf)
    ang = jnp.concatenate([row[:, None] * inv, col[:, None] * inv], axis=-1)
    cos, sin = jnp.cos(ang), jnp.sin(ang)
    lane = np.arange(LANES)
    active = (lane >= lane_lo) & (lane < lane_hi)
    pair = np.where(active, ((lane - lane_lo) % period) // 2, 0)
    even = (lane % 2) == 0
    c = jnp.where(active[None, :], cos[:, pair], 1.0)
    sa = jnp.where((active & even)[None, :], -sin[:, pair], 0.0)
    sb = jnp.where((active & ~even)[None, :], sin[:, pair], 0.0)
    return c, sa, sb


def _pad_cols(w, heads, width):
    k = w.shape[0]
    w = w.reshape(k, heads, width)
    return jnp.pad(w, ((0, 0), (0, 0), (0, LANES - width))).reshape(k, heads * LANES)


def kernel(x, g_attn_norm, w_in, b_gate, g_q_lat, w_q_up, g_kv_lat, w_kv_up, g_mla_qnorm, g_mla_knorm, g_gqa_qnorm, g_gqa_knorm, w_mla_branch, w_gqa_branch, w_out, g_ffn_norm, w_router, b_router, w_exp_gate, w_exp_up, w_exp_down):
    b, s, d = x.shape
    assert d == D_MODEL and s % max(PRE_TM, ROUTE_BLK) == 0 and ROUTE_BLK % MOE_TC == 0
    assert s % (MOE_TC * MOE_STEP_CHUNKS) == 0 and (b * s) % POST_TM == 0
    cap = CAPACITY_FACTOR * s // N_EXPERTS
    assert cap >= MOE_WIN and cap % SLOT_ALIGN == 0
    t = b * s

    splits = np.cumsum([Q_LORA, KV_LORA, MLA_ROPE, GQA_HEADS * GQA_HEAD_DIM,
                        GQA_KV_HEADS * GQA_HEAD_DIM, GQA_KV_HEADS * GQA_HEAD_DIM, D_MODEL])
    w_ql, w_kvl, w_kr, w_qg, w_kg, w_vg, w_ga, w_gb = jnp.split(w_in, splits, axis=1)
    w_kr = jnp.pad(w_kr, ((0, 0), (MLA_NOPE, LANES - MLA_QK)))
    w_kg = w_kg.reshape(d, GQA_KV_HEADS, 1, GQA_HEAD_DIM)
    w_kg = jnp.concatenate([w_kg, w_kg], axis=2).reshape(d, GQA_KV_HEADS * LANES)
    w_vg = w_vg.reshape(d, GQA_KV_HEADS, 1, GQA_HEAD_DIM)
    zeros = jnp.zeros_like(w_vg)
    w_vg = jnp.concatenate([jnp.concatenate([w_vg, zeros], axis=3),
                            jnp.concatenate([zeros, w_vg], axis=3)], axis=2).reshape(d, -1)
    win = jnp.concatenate([w_ql, w_kvl, w_kr, w_qg, w_kg, w_vg, w_ga, w_gb], axis=1).astype(BF16)
    assert win.shape[1] == _C_END
    wqup = _pad_cols(w_q_up, MLA_HEADS, MLA_QK).astype(BF16)
    v_even = w_kv_up.reshape(KV_LORA, MLA_HEADS, LANES)[:, 0::2, MLA_NOPE:]
    wkvup = jnp.concatenate([w_kv_up, _pad_cols(v_even.reshape(KV_LORA, -1), MLA_HEADS // 2, MLA_V)],
                            axis=1).astype(BF16)

    def row(v):
        return v.reshape(1, -1).astype(F32)

    gq = row(jnp.pad(g_mla_qnorm, (0, LANES - MLA_QK)))
    gk = row(jnp.pad(g_mla_knorm, (0, LANES - MLA_QK)))
    ggq = row(jnp.tile(g_gqa_qnorm, 2))
    ggk = row(jnp.tile(g_gqa_knorm, 2))

    assert MLA_V == HALF and GQA_HEAD_DIM == HALF
    wa = w_mla_branch.astype(BF16)
    wb = w_gqa_branch.astype(BF16)

    wr = jnp.pad(w_router, ((0, 0), (0, LANES - N_EXPERTS)))
    wr_hi = wr.astype(BF16)
    wr_lo = (wr - wr_hi.astype(F32)).astype(BF16)
    br = jnp.concatenate([b_router.astype(F32), jnp.full((LANES - N_EXPERTS,), -1e30, F32)]).reshape(1, LANES)

    tabs_a = _rope_tables(s, MLA_ROPE, MLA_NOPE, MLA_QK, MLA_ROPE)
    tabs_b = _rope_tables(s, GQA_HEAD_DIM, 0, LANES, GQA_HEAD_DIM)

    x2d = x.reshape(t, d)
    qa, ka, va, qb, kb, vb, ga, gb = _pre_call(
        x2d, s, row(g_attn_norm), win, row(g_q_lat), wqup, row(g_kv_lat), wkvup,
        gk, ggk, row(b_gate), tabs_a + tabs_b)

    def seq(a):
        return a.reshape(b, s, a.shape[-1])

    ya = _flash_call(seq(qa), gq, tabs_a, seq(ka), seq(va), MLA_HEADS, 2, ((0, None), (1, None)),
                     (0, 1), (0, 1), MLA_QK, "attention_mla")
    yb = _flash_call(seq(qb), ggq, tabs_b, seq(kb), seq(vb), GQA_HEADS, GQA_GROUP,
                     tuple((g // 2, g % 2) for g in range(GQA_GROUP)), (0,) * GQA_GROUP,
                     tuple(g % 2 for g in range(GQA_GROUP)), GQA_HEAD_DIM, "attention_gqa")

    x1, h2, aff, aff_rows = _post_call(x2d, ya.reshape(t, -1), yb.reshape(t, -1), ga, gb,
                                       wa, wb, w_out.astype(BF16), row(g_ffn_norm), wr_hi, wr_lo, br)

    slot_t, gate_t, slot_r, meta = _route_call(seq(aff), aff_rows, cap)
    meta = meta.reshape(b, s // MOE_TC, 1, LANES)
    xe = _dispatch_call(meta, slot_r, seq(h2), cap)
    y = _expert_call(xe, w_exp_gate, w_exp_up, w_exp_down, cap)
    return _combine_call(meta, seq(x1), slot_t, gate_t, y, cap)
```
